```python
import math
import jax, jax.numpy as jnp
from jax import lax
import numpy as np

D_MODEL = 2048
BATCH = 2
SEQ = 8192
DEPTH = 1
DEC_BATCH = 32
DEC_SEQ = 4
PAST_LEN = 16384
PAGE_SIZE = 128

N_META = 16
MIX_WIDTH = D_MODEL
GLA_WIDTH = MIX_WIDTH // 2
GLA_HEADS = 4
GLA_DV = GLA_WIDTH // GLA_HEADS
GLA_DK = GLA_DV // 2
GLA_GATE_RANK = 16
GLA_GATE_NORM = 16.0
GLA_CHUNK = 64
DIFF_WIDTH = MIX_WIDTH - GLA_WIDTH
DIFF_HEADS = 4
DIFF_DV = DIFF_WIDTH // DIFF_HEADS
DIFF_HD = DIFF_DV // 2
ROT_DIM = DIFF_HD // 4
ROPE_THETA = 500000.0
Q_BLOCK = 128
D_FF = ((8 * D_MODEL // 3 + 255) // 256) * 256
EPS = 1e-6
COL_SIZES = (GLA_HEADS * GLA_DK, GLA_HEADS * GLA_DK, GLA_WIDTH, GLA_GATE_RANK, GLA_WIDTH,
             DIFF_HEADS * 2 * DIFF_HD, DIFF_HEADS * 2 * DIFF_HD, DIFF_WIDTH)
D_IN_COLS = sum(COL_SIZES)

kernel_name = "hymba_gla_diffattn_step"

F32 = jnp.float32


def rmsnorm(x, w):
    xf = x.astype(F32)
    r = lax.rsqrt(jnp.mean(xf * xf, axis=-1, keepdims=True) + EPS)
    return (xf * r * w.astype(F32)).astype(x.dtype)


def rope_partial(x, pos):
    half = ROT_DIM // 2
    inv = ROPE_THETA ** (-jnp.arange(half, dtype=F32) * 2.0 / ROT_DIM)
    ang = pos.astype(F32)[:, None] * inv[None, :]
    cos = jnp.cos(ang)[:, None, None, :]
    sin = jnp.sin(ang)[:, None, None, :]
    x1 = x[..., :half].astype(F32)
    x2 = x[..., half:ROT_DIM].astype(F32)
    rot = jnp.concatenate([x1 * cos - x2 * sin, x2 * cos + x1 * sin], axis=-1).astype(x.dtype)
    return jnp.concatenate([rot, x[..., ROT_DIM:]], axis=-1)


def project(h, w_in, w_a2, b_a):
    b, t = h.shape[:2]
    splits = np.cumsum(COL_SIZES)[:-1].tolist()
    q_g, k_g, v_g, a_lr, g_out, q_d, k_d, v_d = jnp.split(h @ w_in, splits, axis=-1)
    q_g = q_g.reshape(b, t, GLA_HEADS, GLA_DK) * (GLA_DK ** -0.5)
    k_g = k_g.reshape(b, t, GLA_HEADS, GLA_DK)
    v_g = v_g.reshape(b, t, GLA_HEADS, GLA_DV)
    gk = jax.nn.log_sigmoid((a_lr @ w_a2 + b_a).astype(F32)) / GLA_GATE_NORM
    gk = gk.reshape(b, t, GLA_HEADS, GLA_DK)
    q_d = q_d.reshape(b, t, DIFF_HEADS, 2, DIFF_HD)
    k_d = k_d.reshape(b, t, DIFF_HEADS, 2, DIFF_HD)
    v_d = v_d.reshape(b, t, DIFF_HEADS, DIFF_DV)
    return q_g, k_g, v_g, gk, g_out, q_d, k_d, v_d


def gla_chunk(s0, q, k, v, gk):
    c = q.shape[1]
    bcum = jnp.cumsum(gk, axis=1)
    b_ref = bcum[:, c // 2][:, None]
    b_last = bcum[:, -1]
    qf, kf, vf = q.astype(F32), k.astype(F32), v.astype(F32)
    a = jnp.einsum('bthk,bshk->bhts', qf * jnp.exp(bcum - b_ref), kf * jnp.exp(b_ref - bcum))
    a = jnp.where(jnp.tril(jnp.ones((c, c), bool)), a, 0.0)
    o = jnp.einsum('bhts,bshv->bthv', a, vf) + jnp.einsum('bthk,bhkv->bthv', qf * jnp.exp(bcum), s0)
    s_new = jnp.exp(b_last)[..., None] * s0 + jnp.einsum('bshk,bshv->bhkv', kf * jnp.exp(b_last[:, None] - bcum), vf)
    return o, s_new


def gla_prompt(q, k, v, gk):
    b = q.shape[0]
    s0 = jnp.zeros((b, GLA_HEADS, GLA_DK, GLA_DV), F32)
    o_meta, s = gla_chunk(s0, q[:, :N_META], k[:, :N_META], v[:, :N_META], gk[:, :N_META])

    def chunks(t):
        r = t[:, N_META:]
        return r.reshape((b, -1, GLA_CHUNK) + r.shape[2:]).swapaxes(0, 1)

    def step(state, xs):
        o, state = gla_chunk(state, *xs)
        return state, o

    s, o_rest = lax.scan(step, s, (chunks(q), chunks(k), chunks(v), chunks(gk)))
    o_rest = o_rest.swapaxes(0, 1).reshape(b, -1, GLA_HEADS, GLA_DV)
    return jnp.concatenate([o_meta, o_rest], axis=1), s


def diff_scores(q, k):
    return jnp.einsum('bqhcd,bkhcd->bhcqk', q.astype(F32), k.astype(F32)) * (DIFF_HD ** -0.5)


def diff_combine(p, v, lam):
    a = p[:, :, 0] - lam * p[:, :, 1]
    return jnp.einsum('bhqk,bkhv->bqhv', a, v.astype(F32))


def diff_prompt(q, k, v, lam):
    b, seq_len = q.shape[:2]
    n_blocks = (seq_len - N_META) // Q_BLOCK
    meta_mask = jnp.tril(jnp.ones((N_META, N_META), bool))
    s = jnp.where(meta_mask, diff_scores(q[:, :N_META], k[:, :N_META]), -jnp.inf)
    o_meta = diff_combine(jax.nn.softmax(s, axis=-1), v[:, :N_META], lam)
    kpos = jnp.arange(seq_len)

    def block(j):
        start = N_META + j * Q_BLOCK
        qb = lax.dynamic_slice_in_dim(q, start, Q_BLOCK, axis=1)
        qpos = start + jnp.arange(Q_BLOCK)
        s = jnp.where(kpos[None, :] <= qpos[:, None], diff_scores(qb, k), -jnp.inf)
        return diff_combine(jax.nn.softmax(s, axis=-1), v, lam)

    o = lax.map(block, jnp.arange(n_blocks))
    o = o.swapaxes(0, 1).reshape(b, seq_len - N_META, DIFF_HEADS, DIFF_DV)
    return jnp.concatenate([o_meta, o], axis=1)


def softmax_update(carry, s, vb):
    m, l, acc = carry
    m_new = jnp.maximum(m, jnp.max(s, axis=-1))
    corr = jnp.exp(m - m_new)
    p = jnp.exp(s - m_new[..., None])
    l = l * corr + jnp.sum(p, axis=-1)
    acc = acc * corr[..., None] + jnp.einsum('bhcqk,bkhv->bhcqv', p, vb.astype(F32))
    return (m_new, l, acc)


def diff_sample(q, k_new, v_new, lam, cache_k, cache_v, layer, page_table):
    bd, t = q.shape[:2]
    carry = (jnp.full((bd, DIFF_HEADS, 2, t), -1e30, F32),
             jnp.zeros((bd, DIFF_HEADS, 2, t), F32),
             jnp.zeros((bd, DIFF_HEADS, 2, t, DIFF_DV), F32))

    def page_step(c, pages):
        kb = cache_k[layer, pages]
        vb = cache_v[layer, pages]
        return softmax_update(c, diff_scores(q, kb), vb), None

    carry, _ = lax.scan(page_step, carry, page_table.T)
    causal = jnp.tril(jnp.ones((t, t), bool))
    s = jnp.where(causal, diff_scores(q, k_new), -jnp.inf)
    m, l, acc = softmax_update(carry, s, v_new)
    out = acc / l[..., None]
    a = out[:, :, 0] - lam * out[:, :, 1]
    return a.transpose(0, 2, 1, 3)


def layer(x, pos, lam, lam_init, norm1_w, w_in, w_a2, b_a, gla_norm_w, diff_norm_w, w_o,
          norm2_w, w_gate, w_up, w_down, gla_fn, diff_fn):
    b, t = x.shape[:2]
    h = rmsnorm(x, norm1_w)
    q_g, k_g, v_g, gk, g_out, q_d, k_d, v_d = project(h, w_in, w_a2, b_a)
    q_d = rope_partial(q_d, pos)
    k_d = rope_partial(k_d, pos)
    o_g, s_gla = gla_fn(q_g, k_g, v_g, gk)
    o_d = diff_fn(q_d, k_d, v_d, lam)
    o_g = rmsnorm(o_g, gla_norm_w).reshape(b, t, GLA_WIDTH) * jax.nn.silu(g_out.astype(F32))
    o_d = rmsnorm(o_d, diff_norm_w).reshape(b, t, DIFF_WIDTH) * (1.0 - lam_init)
    mix = jnp.concatenate([o_g, o_d], axis=-1).astype(x.dtype)
    x = x + mix @ w_o
    h2 = rmsnorm(x, norm2_w)
    x = x + (jax.nn.silu(h2 @ w_gate) * (h2 @ w_up)) @ w_down
    return x, k_d, v_d, s_gla


def setup_inputs(seed: int = 0) -> dict:
    key = jax.random.key(seed)
    ks = jax.random.split(key, 32)

    def nrm(k, shape, scale):
        return jax.random.normal(k, shape, F32) * scale

    n_pages = PAST_LEN // PAGE_SIZE
    n_used = DEC_BATCH * n_pages
    n_phys = n_used + max(1, n_used // 4)
    page_table = jax.random.permutation(ks[5], n_phys)[:n_used].astype(jnp.int32).reshape(DEC_BATCH, n_pages)
    return {
        'x_prompt': nrm(ks[0], (BATCH, SEQ, D_MODEL), 1.0),
        'x_sample': nrm(ks[1], (DEC_BATCH, DEC_SEQ, D_MODEL), 1.0),
        'cache_k': nrm(ks[2], (DEPTH, n_phys, PAGE_SIZE, DIFF_HEADS, 2, DIFF_HD), 1.0),
        'cache_v': nrm(ks[3], (DEPTH, n_phys, PAGE_SIZE, DIFF_HEADS, DIFF_DV), 1.0),
        'state_gla': nrm(ks[4], (DEPTH, DEC_BATCH, GLA_HEADS, GLA_DK, GLA_DV), 2.0),
        'page_table': page_table,
        'meta_tokens': nrm(ks[6], (N_META, D_MODEL), 1.0),
        'norm1_w': 1.0 + nrm(ks[7], (DEPTH, D_MODEL), 0.02),
        'w_in': nrm(ks[8], (DEPTH, D_MODEL, D_IN_COLS), D_MODEL ** -0.5),
        'w_a2': nrm(ks[9], (DEPTH, GLA_GATE_RANK, GLA_HEADS * GLA_DK), GLA_GATE_RANK ** -0.5),
        'b_a': nrm(ks[10], (DEPTH, GLA_HEADS * GLA_DK), 0.1),
        'gla_norm_w': 1.0 + nrm(ks[11], (DEPTH, GLA_DV), 0.02),
        'lambda_q1': nrm(ks[12], (DEPTH, DIFF_HD), 0.1),
        'lambda_k1': nrm(ks[13], (DEPTH, DIFF_HD), 0.1),
        'lambda_q2': nrm(ks[14], (DEPTH, DIFF_HD), 0.1),
        'lambda_k2': nrm(ks[15], (DEPTH, DIFF_HD), 0.1),
        'diff_norm_w': 1.0 + nrm(ks[16], (DEPTH, DIFF_DV), 0.02),
        'w_o': nrm(ks[17], (DEPTH, MIX_WIDTH, D_MODEL), MIX_WIDTH ** -0.5),
        'norm2_w': 1.0 + nrm(ks[18], (DEPTH, D_MODEL), 0.02),
        'w_gate': nrm(ks[19], (DEPTH, D_MODEL, D_FF), D_MODEL ** -0.5),
        'w_up': nrm(ks[20], (DEPTH, D_MODEL, D_FF), D_MODEL ** -0.5),
        'w_down': nrm(ks[21], (DEPTH, D_FF, D_MODEL), D_FF ** -0.5),
        'final_norm_w': 1.0 + nrm(ks[22], (D_MODEL,), 0.02),
    }


def reference(x_prompt, x_sample, cache_k, cache_v, state_gla, page_table, meta_tokens, norm1_w, w_in,
              w_a2, b_a, gla_norm_w, lambda_q1, lambda_k1, lambda_q2, lambda_k2, diff_norm_w, w_o,
              norm2_w, w_gate, w_up, w_down, final_norm_w):
    b = x_prompt.shape[0]
    past_len = page_table.shape[1] * PAGE_SIZE
    meta = jnp.broadcast_to(meta_tokens.astype(x_prompt.dtype)[None], (b, N_META, D_MODEL))
    x_p = jnp.concatenate([meta, x_prompt], axis=1)
    x_s = x_sample
    pos_p = jnp.arange(x_p.shape[1])
    pos_s = past_len + jnp.arange(x_s.shape[1])
    kp, vp, sp, ksm, vsm, ssm = [], [], [], [], [], []
    for l in range(DEPTH):
        lam_init = 0.8 - 0.6 * math.exp(-0.3 * l)
        lam = (jnp.exp(jnp.sum(lambda_q1[l].astype(F32) * lambda_k1[l].astype(F32)))
               - jnp.exp(jnp.sum(lambda_q2[l].astype(F32) * lambda_k2[l].astype(F32))) + lam_init)
        weights = (norm1_w[l], w_in[l], w_a2[l], b_a[l], gla_norm_w[l], diff_norm_w[l], w_o[l],
                   norm2_w[l], w_gate[l], w_up[l], w_down[l])
        x_p, k_new, v_new, s_new = layer(x_p, pos_p, lam, lam_init, *weights, gla_prompt, diff_prompt)
        kp.append(k_new)
        vp.append(v_new)
        sp.append(s_new.astype(state_gla.dtype))
        s0 = state_gla[l].astype(F32)
        gla_fn = lambda q, k, v, gk, s0=s0: gla_chunk(s0, q, k, v, gk)
        diff_fn = lambda q, k, v, lm, l=l: diff_sample(q, k, v, lm, cache_k, cache_v, l, page_table)
        x_s, k_new, v_new, s_new = layer(x_s, pos_s, lam, lam_init, *weights, gla_fn, diff_fn)
        ksm.append(k_new)
        vsm.append(v_new)
        ssm.append(s_new.astype(state_gla.dtype))
    y_prompt = rmsnorm(x_p, final_norm_w)[:, N_META:]
    y_sample = rmsnorm(x_s, final_norm_w)
    return (y_prompt, y_sample, jnp.stack(kp), jnp.stack(vp), jnp.stack(sp),
            jnp.stack(ksm), jnp.stack(vsm), jnp.stack(ssm))
```

```python
import functools
import math

import jax
import jax.numpy as jnp
from jax import lax
from jax.experimental import pallas as pl
from jax.experimental.pallas import tpu as pltpu

F32 = jnp.float32
BF16 = jnp.bfloat16

N_META = 16
GLA_HEADS = 4
GLA_DK = 128
GLA_DV = 256
GLA_GATE_RANK = 16
GLA_GATE_NORM = 16.0
GLA_CHUNK = 64
DIFF_HEADS = 4
DIFF_HD = 128
DIFF_DV = 256
ROT_DIM = DIFF_HD // 4
ROPE_THETA = 500000.0
PAGE_SIZE = 128
EPS = 1e-6
NEG_BIG = -1e30

LANES = 128
VMEM_LIMIT = 56 * 1024 * 1024


def _dot(a, b):
    return jnp.dot(a, b, preferred_element_type=F32)


def _dot_nt(a, b):
    return lax.dot_general(a, b, (((1,), (1,)), ((), ())), preferred_element_type=F32)


def _dot_tn(a, b):
    return lax.dot_general(a, b, (((0,), (0,)), ((), ())), preferred_element_type=F32)


def _rms(x):
    return lax.rsqrt(jnp.mean(x * x, axis=-1, keepdims=True) + EPS)


def _silu(x):
    return x * jax.nn.sigmoid(x)


def _lambda(lq1, lk1, lq2, lk2, lam_init):
    return (jnp.exp(jnp.sum(lq1[...] * lk1[...], axis=-1, keepdims=True))
            - jnp.exp(jnp.sum(lq2[...] * lk2[...], axis=-1, keepdims=True)) + lam_init)


def _proj_kernel(x_ref, n1_ref, w_ref, wa_ref, wa2_ref, ba_ref, c_ref, s1_ref, s2_ref,
                 qg_ref, kg_ref, vg_ref, gk_ref, go_ref, qd_ref, kd_ref, vd_ref, kf_ref, vf_ref):
    x = x_ref[...]
    h = (x * _rms(x) * n1_ref[...]).astype(BF16)
    cw = 512

    def mm(c):
        return _dot(h, w_ref[:, c * cw:(c + 1) * cw])

    qg_ref[...] = (mm(0) * (GLA_DK ** -0.5)).astype(BF16)
    kg_ref[...] = mm(1).astype(BF16)
    for c in range(2):
        vg_ref[:, c * cw:(c + 1) * cw] = mm(2 + c).astype(BF16)
        go_ref[:, c * cw:(c + 1) * cw] = mm(4 + c).astype(BF16)

    a_lr = _dot(h, wa_ref[...]).astype(BF16)
    z = _dot(a_lr, wa2_ref[...]) + ba_ref[...]
    gk_ref[...] = (jnp.minimum(z, 0.0) - jnp.log(1.0 + jnp.exp(-jnp.abs(z)))) * (1.0 / GLA_GATE_NORM)

    reps = cw // LANES
    cos = jnp.tile(c_ref[...], (1, reps))
    sin_lo = jnp.tile(s1_ref[...], (1, reps))
    sin_hi = jnp.tile(s2_ref[...], (1, reps))

    def rope(v):
        return (v * cos + pltpu.roll(v, ROT_DIM // 2, 1) * sin_lo
                + pltpu.roll(v, cw - ROT_DIM // 2, 1) * sin_hi)

    for c in range(2):
        cols = slice(c * cw, (c + 1) * cw)
        qd_ref[:, cols] = (rope(mm(6 + c)) * (DIFF_HD ** -0.5)).astype(BF16)
        kr = rope(mm(8 + c))
        kf_ref[:, cols] = kr
        kd_ref[:, cols] = kr.astype(BF16)
        vv = mm(10 + c)
        vf_ref[:, cols] = vv
        vd_ref[:, cols] = vv.astype(BF16)


def _proj(x, n1, w, wa, wa2, ba, cos, s1, s2, tm):
    r, d = x.shape
    ncol = w.shape[1]
    row = lambda width: pl.BlockSpec((tm, width), lambda i: (i, 0))
    full = lambda a: pl.BlockSpec(a.shape, lambda i: (0, 0))
    widths = (512, 512, 1024, 512, 1024, 1024, 1024, 1024, 1024, 1024)
    dtypes = (BF16, BF16, BF16, F32, BF16, BF16, BF16, BF16, F32, F32)
    return pl.pallas_call(
        _proj_kernel,
        grid=(r // tm,),
        in_specs=[row(d), full(n1),
                  pl.BlockSpec((d, ncol), lambda i: (0, 0), pipeline_mode=pl.Buffered(1)),
                  full(wa), full(wa2), full(ba), row(LANES), row(LANES), row(LANES)],
        out_specs=[row(wd) for wd in widths],
        out_shape=[jax.ShapeDtypeStruct((r, wd), dt) for wd, dt in zip(widths, dtypes)],
        compiler_params=pltpu.CompilerParams(dimension_semantics=("parallel",),
                                             vmem_limit_bytes=VMEM_LIMIT),
        name="proj",
    )(x, n1, w, wa, wa2, ba, cos, s1, s2)


def _gla_kernel(q_ref, k_ref, v_ref, gk_ref, go_ref, s0_ref, nw_ref, o_ref, st_ref, s_scr,
                *, chunk, n_chunks):
    t = pl.program_id(1)

    @pl.when(t == 0)
    def _():
        s_scr[...] = s0_ref[...]

    c = chunk
    row = lax.broadcasted_iota(jnp.int32, (c, c), 0)
    col = lax.broadcasted_iota(jnp.int32, (c, c), 1)
    tril = col <= row
    tri = jnp.where(tril, 1.0, 0.0).astype(BF16)

    def chunk_body(ci, carry):
        rows = pl.ds(pl.multiple_of(ci * c, c), c)
        g = gk_ref[rows, :]
        g_hi = g.astype(BF16)
        g_lo = (g - g_hi.astype(F32)).astype(BF16)
        bcum = _dot(tri, g_hi) + _dot(tri, g_lo)
        b_mid = bcum[c // 2:c // 2 + 1, :]
        b_last = bcum[c - 1:c, :]
        q = q_ref[rows, :].astype(F32)
        k = k_ref[rows, :].astype(F32)
        qe = (q * jnp.exp(bcum - b_mid)).astype(BF16)
        ke = (k * jnp.exp(b_mid - bcum)).astype(BF16)
        qs = (q * jnp.exp(bcum)).astype(BF16)
        kl = (k * jnp.exp(b_last - bcum)).astype(BF16)
        decay = jnp.exp(b_last)
        for h in range(GLA_HEADS):
            hk = slice(h * GLA_DK, (h + 1) * GLA_DK)
            hv = slice(h * GLA_DV, (h + 1) * GLA_DV)
            vh = v_ref[rows, hv]
            a = jnp.where(tril, _dot_nt(qe[:, hk], ke[:, hk]), 0.0)
            st = s_scr[h]
            o = _dot(a.astype(BF16), vh) + _dot_nt(qs[:, hk], st.astype(BF16))
            s_scr[h] = decay[:, hk] * st + _dot_tn(vh, kl[:, hk])
            gate = go_ref[rows, hv].astype(F32)
            o_ref[rows, hv] = (o * _rms(o) * nw_ref[...] * _silu(gate)).astype(BF16)
        return carry

    lax.fori_loop(0, n_chunks, chunk_body, 0)

    @pl.when(t == pl.num_programs(1) - 1)
    def _():
        st_ref[...] = s_scr[...]


def _gla(q, k, v, gk, go, s0t, nw, chunk, n_chunks):
    b, t, _ = q.shape
    tc = chunk * n_chunks
    seq = lambda width: pl.BlockSpec((None, tc, width), lambda i, j: (i, j, 0))
    state = pl.BlockSpec((None, GLA_HEADS, GLA_DV, GLA_DK), lambda i, j: (i, 0, 0, 0))
    return pl.pallas_call(
        functools.partial(_gla_kernel, chunk=chunk, n_chunks=n_chunks),
        grid=(b, t // tc),
        in_specs=[seq(512), seq(512), seq(1024), seq(512), seq(1024), state,
                  pl.BlockSpec(nw.shape, lambda i, j: (0, 0))],
        out_specs=[seq(1024), state],
        out_shape=[jax.ShapeDtypeStruct((b, t, 1024), BF16),
                   jax.ShapeDtypeStruct((b, GLA_HEADS, GLA_DV, GLA_DK), F32)],
        scratch_shapes=[pltpu.VMEM((GLA_HEADS, GLA_DV, GLA_DK), F32)],
        compiler_params=pltpu.CompilerParams(dimension_semantics=("parallel", "arbitrary"),
                                             vmem_limit_bytes=VMEM_LIMIT),
        name="gla",
    )(q, k, v, gk, go, s0t, nw)


def _diff_prompt_kernel(q_ref, k_ref, v_ref, km_ref, vm_ref, lq1, lk1, lq2, lk2, nw_ref, o_ref,
                        m_scr, l_scr, acc_scr, *, tq, lam_init):
    qi = pl.program_id(2)
    q = q_ref[...]

    def update(c, s, vb, first=False):
        m_cur = jnp.max(s, axis=-1, keepdims=True)
        if first:
            m_new = m_cur
            p = jnp.exp(s - m_new)
            l_scr[c] = jnp.broadcast_to(jnp.sum(p, axis=-1, keepdims=True), (tq, LANES))
            acc_scr[c] = _dot(p.astype(BF16), vb)
        else:
            m_prev = m_scr[c][:, :1]
            m_new = jnp.maximum(m_prev, m_cur)
            alpha = jnp.exp(m_prev - m_new)
            p = jnp.exp(s - m_new)
            l_scr[c] = alpha * l_scr[c] + jnp.sum(p, axis=-1, keepdims=True)
            acc_scr[c] = alpha * acc_scr[c] + _dot(p.astype(BF16), vb)
        m_scr[c] = jnp.broadcast_to(m_new, (tq, LANES))

    for c in range(2):
        cs = slice(c * DIFF_HD, (c + 1) * DIFF_HD)
        update(c, _dot_nt(q[:, cs], km_ref[:, cs]), vm_ref[...], first=True)

    def block(j, masked):
        rows = pl.ds(pl.multiple_of(j * tq, tq), tq)
        kb = k_ref[rows, :]
        vb = v_ref[rows, :]
        for c in range(2):
            cs = slice(c * DIFF_HD, (c + 1) * DIFF_HD)
            s = _dot_nt(q[:, cs], kb[:, cs])
            if masked:
                r = lax.broadcasted_iota(jnp.int32, (tq, tq), 0)
                cc = lax.broadcasted_iota(jnp.int32, (tq, tq), 1)
                s = jnp.where(cc <= r, s, NEG_BIG)
            update(c, s, vb)

    def body(j, carry):
        block(j, False)
        return carry

    lax.fori_loop(0, qi, body, 0)
    block(qi, True)

    lam = _lambda(lq1, lk1, lq2, lk2, lam_init)
    a1 = acc_scr[0] / l_scr[0][:, :1]
    a2 = acc_scr[1] / l_scr[1][:, :1]
    o = a1 - lam * a2
    o_ref[...] = (o * _rms(o) * nw_ref[...] * (1.0 - lam_init)).astype(BF16)


def _diff_prompt(q, k, v, km, vm, lams, nw, lam_init, tq):
    b, s, _ = q.shape
    hw = 2 * DIFF_HD
    small = lambda a: pl.BlockSpec(a.shape, lambda i, h, j: (0, 0))
    return pl.pallas_call(
        functools.partial(_diff_prompt_kernel, tq=tq, lam_init=lam_init),
        grid=(b, DIFF_HEADS, s // tq),
        in_specs=[pl.BlockSpec((None, tq, hw), lambda i, h, j: (i, j, h)),
                  pl.BlockSpec((None, s, hw), lambda i, h, j: (i, 0, h)),
                  pl.BlockSpec((None, s, DIFF_DV), lambda i, h, j: (i, 0, h)),
                  pl.BlockSpec((N_META, hw), lambda i, h, j: (0, h)),
                  pl.BlockSpec((N_META, DIFF_DV), lambda i, h, j: (0, h)),
                  small(lams[0]), small(lams[1]), small(lams[2]), small(lams[3]), small(nw)],
        out_specs=pl.BlockSpec((None, tq, DIFF_DV), lambda i, h, j: (i, j, h)),
        out_shape=jax.ShapeDtypeStruct((b, s, DIFF_HEADS * DIFF_DV), BF16),
        scratch_shapes=[pltpu.VMEM((2, tq, LANES), F32), pltpu.VMEM((2, tq, LANES), F32),
                        pltpu.VMEM((2, tq, DIFF_DV), F32)],
        compiler_params=pltpu.CompilerParams(
            dimension_semantics=("parallel", "parallel", "arbitrary"),
            vmem_limit_bytes=VMEM_LIMIT),
        name="diff_prompt",
    )(q, k, v, km, vm, *lams, nw)


Q_ROWS = 8
NEW_PAD = 16


def _diff_sample_kernel(pt_ref, q_ref, *refs, group, n_new, lam_init):
    k_refs = refs[:group]
    v_refs = refs[group:2 * group]
    kn_ref, vn_ref, lq1, lk1, lq2, lk2, nw_ref, o_ref, m_scr, l_scr, acc_scr = refs[2 * group:]
    g = pl.program_id(1)

    @pl.when(g == 0)
    def _():
        m_scr[...] = jnp.full(m_scr.shape, NEG_BIG, F32)
        l_scr[...] = jnp.zeros(l_scr.shape, F32)
        acc_scr[...] = jnp.zeros(acc_scr.shape, F32)

    q = q_ref[...]

    def update(s, vb):
        m_prev = m_scr[...][:, :1]
        m_new = jnp.maximum(m_prev, jnp.max(s, axis=-1, keepdims=True))
        alpha = jnp.exp(m_prev - m_new)
        p = jnp.exp(s - m_new)
        l_scr[...] = alpha * l_scr[...] + jnp.sum(p, axis=-1, keepdims=True)
        acc_scr[...] = alpha * acc_scr[...] + _dot(p.astype(BF16), vb)
        m_scr[...] = jnp.broadcast_to(m_new, m_scr.shape)

    kg = jnp.concatenate([r[...].astype(BF16) for r in k_refs], axis=0)
    vg = jnp.concatenate([r[...].astype(BF16) for r in v_refs], axis=0)
    update(_dot_nt(q, kg), vg)

    @pl.when(g == pl.num_programs(1) - 1)
    def _():
        s = _dot_nt(q, kn_ref[...])
        t = lax.broadcasted_iota(jnp.int32, s.shape, 0) % Q_ROWS
        j = lax.broadcasted_iota(jnp.int32, s.shape, 1)
        s = jnp.where(j <= jnp.minimum(t, n_new - 1), s, NEG_BIG)
        update(s, vn_ref[...])
        lam = _lambda(lq1, lk1, lq2, lk2, lam_init)
        out = acc_scr[...] / l_scr[...][:, :1]
        for h in range(DIFF_HEADS):
            r0 = h * 2 * Q_ROWS
            cols = slice(h * DIFF_DV, (h + 1) * DIFF_DV)
            o = out[r0:r0 + Q_ROWS, cols] - lam * out[r0 + Q_ROWS:r0 + 2 * Q_ROWS, cols]
            o_ref[:, cols] = o * _rms(o) * nw_ref[...] * (1.0 - lam_init)


def _diff_sample(page_table, qbd, cache_k, cache_v, k_new, v_new, lams, nw, lam_init, n_new, group):
    bd, n_pages = page_table.shape
    nrow, kw = qbd.shape[1:]
    vw = cache_v.shape[-1]
    n_groups = n_pages // group

    def page_map(b, g, pt, i):
        return (pt[b, g * group + i], 0, 0)

    per_b = lambda a: pl.BlockSpec((None,) + a.shape[1:], lambda b, g, pt: (b, 0, 0))
    small = lambda a: pl.BlockSpec(a.shape, lambda b, g, pt: (0, 0))
    in_specs = ([per_b(qbd)]
                + [pl.BlockSpec((None, PAGE_SIZE, kw), functools.partial(page_map, i=i))
                   for i in range(group)]
                + [pl.BlockSpec((None, PAGE_SIZE, vw), functools.partial(page_map, i=i))
                   for i in range(group)]
                + [per_b(k_new), per_b(v_new)] + [small(a) for a in lams] + [small(nw)])
    return pl.pallas_call(
        functools.partial(_diff_sample_kernel, group=group, n_new=n_new, lam_init=lam_init),
        grid_spec=pltpu.PrefetchScalarGridSpec(
            num_scalar_prefetch=1,
            grid=(bd, n_groups),
            in_specs=in_specs,
            out_specs=pl.BlockSpec((None, Q_ROWS, vw), lambda b, g, pt: (b, 0, 0)),
            scratch_shapes=[pltpu.VMEM((nrow, LANES), F32), pltpu.VMEM((nrow, LANES), F32),
                            pltpu.VMEM((nrow, vw), F32)]),
        out_shape=jax.ShapeDtypeStruct((bd, Q_ROWS, vw), F32),
        compiler_params=pltpu.CompilerParams(dimension_semantics=("parallel", "arbitrary"),
                                             vmem_limit_bytes=VMEM_LIMIT),
        name="diff_sample",
    )(page_table, qbd, *([cache_k] * group), *([cache_v] * group), k_new, v_new, *lams, nw)


def _out_proj_kernel(x_ref, mg_ref, md_ref, wo_ref, n2_ref, x1_ref, h2_ref):
    half = mg_ref.shape[1]
    y = x_ref[...] + _dot(mg_ref[...], wo_ref[:half, :]) + _dot(md_ref[...], wo_ref[half:, :])
    x1_ref[...] = y
    h2_ref[...] = (y * _rms(y) * n2_ref[...]).astype(BF16)


def _out_proj(x, mg, md, wo, n2, tm):
    r, d = x.shape
    row = lambda a: pl.BlockSpec((tm, a.shape[1]), lambda i: (i, 0))
    full = lambda a: pl.BlockSpec(a.shape, lambda i: (0, 0))
    return pl.pallas_call(
        _out_proj_kernel,
        grid=(r // tm,),
        in_specs=[row(x), row(mg), row(md), full(wo), full(n2)],
        out_specs=[row(x), row(x)],
        out_shape=[jax.ShapeDtypeStruct((r, d), F32), jax.ShapeDtypeStruct((r, d), BF16)],
        compiler_params=pltpu.CompilerParams(dimension_semantics=("parallel",),
                                             vmem_limit_bytes=VMEM_LIMIT),
        name="out_proj",
    )(x, mg, md, wo, n2)


def _ffn_kernel(h_ref, wg_ref, wu_ref, wd_ref, x_ref, fw_ref, y_ref, acc_scr):
    j = pl.program_id(1)

    @pl.when(j == 0)
    def _():
        acc_scr[...] = jnp.zeros(acc_scr.shape, F32)

    h = h_ref[...]
    act = (_silu(_dot(h, wg_ref[...])) * _dot(h, wu_ref[...])).astype(BF16)
    acc_scr[...] += _dot(act, wd_ref[...])

    @pl.when(j == pl.num_programs(1) - 1)
    def _():
        y = x_ref[...] + acc_scr[...]
        y_ref[...] = y * _rms(y) * fw_ref[...]


def _ffn(h2, wg, wu, wd, x1, fw, tm, tf):
    r, d = h2.shape
    f = wg.shape[1]
    return pl.pallas_call(
        _ffn_kernel,
        grid=(r // tm, f // tf),
        in_specs=[pl.BlockSpec((tm, d), lambda i, j: (i, 0)),
                  pl.BlockSpec((d, tf), lambda i, j: (0, j)),
                  pl.BlockSpec((d, tf), lambda i, j: (0, j)),
                  pl.BlockSpec((tf, d), lambda i, j: (j, 0)),
                  pl.BlockSpec((tm, d), lambda i, j: (i, 0)),
                  pl.BlockSpec(fw.shape, lambda i, j: (0, 0))],
        out_specs=pl.BlockSpec((tm, d), lambda i, j: (i, 0)),
        out_shape=jax.ShapeDtypeStruct((r, d), F32),
        scratch_shapes=[pltpu.VMEM((tm, d), F32)],
        compiler_params=pltpu.CompilerParams(dimension_semantics=("parallel", "arbitrary"),
                                             vmem_limit_bytes=VMEM_LIMIT),
        name="ffn",
    )(h2, wg, wu, wd, x1, fw)


def _rope_tables(pos):
    half = ROT_DIM // 2
    inv = ROPE_THETA ** (-jnp.arange(half, dtype=F32) * 2.0 / ROT_DIM)
    ang = pos.astype(F32)[:, None] * inv[None, :]
    cos, sin = jnp.cos(ang), jnp.sin(ang)
    n = pos.shape[0]
    zeros = lambda w: jnp.zeros((n, w), F32)
    c = jnp.concatenate([cos, cos, jnp.ones((n, LANES - ROT_DIM), F32)], axis=1)
    s_lo = jnp.concatenate([zeros(half), sin, zeros(LANES - ROT_DIM)], axis=1)
    s_hi = jnp.concatenate([-sin, zeros(LANES - half)], axis=1)
    return c, s_lo, s_hi


def kernel(x_prompt, x_sample, cache_k, cache_v, state_gla, page_table, meta_tokens, norm1_w, w_in,
           w_a2, b_a, gla_norm_w, lambda_q1, lambda_k1, lambda_q2, lambda_k2, diff_norm_w, w_o,
           norm2_w, w_gate, w_up, w_down, final_norm_w):
    b, s, d = x_prompt.shape
    bd, td, _ = x_sample.shape
    assert norm1_w.shape[0] == 1, "single-layer step only"
    assert td <= Q_ROWS and td <= N_META
    n_phys = cache_k.shape[1]
    past_len = page_table.shape[1] * PAGE_SIZE
    lam_init = 0.8 - 0.6 * math.exp(-0.3 * 0)
    n_small = bd * td

    w = w_in[0]
    g0 = 2 * GLA_HEADS * GLA_DK + GLA_HEADS * GLA_DV
    w_main = jnp.concatenate([w[:, :g0], w[:, g0 + GLA_GATE_RANK:]], axis=1).astype(BF16)
    w_a1 = jnp.pad(w[:, g0:g0 + GLA_GATE_RANK], ((0, 0), (0, LANES - GLA_GATE_RANK))).astype(BF16)
    w_a2p = jnp.pad(w_a2[0], ((0, LANES - GLA_GATE_RANK), (0, 0))).astype(BF16)
    row2 = lambda a: a.reshape(1, -1).astype(F32)
    n1, ba, gnw, dnw, n2, fw = (row2(norm1_w[0]), row2(b_a[0]), row2(gla_norm_w[0]),
                                row2(diff_norm_w[0]), row2(norm2_w[0]), row2(final_norm_w))
    lams = tuple(row2(a[0]) for a in (lambda_q1, lambda_k1, lambda_q2, lambda_k2))
    wo, wg, wu, wd = (w_o[0].astype(BF16), w_gate[0].astype(BF16), w_up[0].astype(BF16),
                      w_down[0].astype(BF16))

    x_main = x_prompt.reshape(b * s, d)
    x_new = x_sample.reshape(n_small, d)
    x_small = jnp.concatenate([x_new, meta_tokens.astype(F32)], axis=0)
    pos_main = jnp.tile(N_META + jnp.arange(s), b)
    pos_small = jnp.concatenate([past_len + jnp.arange(n_small) % td, jnp.arange(N_META)])

    proj_main = _proj(x_main, n1, w_main, w_a1, w_a2p, ba, *_rope_tables(pos_main), tm=256)
    proj_small = _proj(x_small, n1, w_main, w_a1, w_a2p, ba, *_rope_tables(pos_small),
                       tm=n_small + N_META)
    qg, kg, vg, gk, go, qd, kd, vd, kf, vf = proj_main
    qg_s, kg_s, vg_s, gk_s, go_s, qd_s, kd_s, vd_s, kf_s, vf_s = proj_small

    def small_seq(a):
        new = jnp.pad(a[:n_small].reshape(bd, td, -1), ((0, 0), (0, N_META - td), (0, 0)))
        return jnp.concatenate([new, a[n_small:][None]], axis=0)

    s0_small = jnp.concatenate([state_gla[0].astype(F32),
                                jnp.zeros((1,) + state_gla.shape[2:], F32)], axis=0)
    mg_small, st_small = _gla(small_seq(qg_s), small_seq(kg_s), small_seq(vg_s), small_seq(gk_s),
                              small_seq(go_s), s0_small.swapaxes(-1, -2), gnw,
                              chunk=N_META, n_chunks=1)
    seq = lambda a: a.reshape(b, s, -1)
    s0_main = jnp.broadcast_to(st_small[bd:], (b,) + st_small.shape[1:])
    mg_main, st_main = _gla(seq(qg), seq(kg), seq(vg), seq(gk), seq(go), s0_main, gnw,
                            chunk=GLA_CHUNK, n_chunks=8)

    md_main = _diff_prompt(seq(qd), seq(kd), seq(vd), kd_s[n_small:], vd_s[n_small:], lams, dnw,
                           lam_init, tq=512)

    n_sub = 2 * DIFF_HEADS
    q4 = jnp.pad(qd_s[:n_small].reshape(bd, td, n_sub, DIFF_HD),
                 ((0, 0), (0, Q_ROWS - td), (0, 0), (0, 0))).transpose(0, 2, 1, 3)
    qbd = (q4[:, :, :, None, :] * jnp.eye(n_sub, dtype=BF16)[None, :, None, :, None])
    qbd = qbd.reshape(bd, n_sub * Q_ROWS, n_sub * DIFF_HD)
    pad_new = lambda a: jnp.pad(a[:n_small].reshape(bd, td, -1), ((0, 0), (0, NEW_PAD - td), (0, 0)))
    od_small = _diff_sample(page_table, qbd,
                            cache_k[0].reshape(n_phys, PAGE_SIZE, n_sub * DIFF_HD),
                            cache_v[0].reshape(n_phys, PAGE_SIZE, DIFF_HEADS * DIFF_DV),
                            pad_new(kd_s), pad_new(vd_s), lams, dnw, lam_init, n_new=td, group=8)
    md_small = od_small[:, :td].reshape(n_small, -1).astype(BF16)

    x1_main, h2_main = _out_proj(x_main, mg_main.reshape(b * s, -1), md_main.reshape(b * s, -1),
                                 wo, n2, tm=256)
    y_main = _ffn(h2_main, wg, wu, wd, x1_main, fw, tm=512, tf=512)
    x1_small, h2_small = _out_proj(x_new, mg_small[:bd, :td].reshape(n_small, -1), md_small, wo, n2,
                                   tm=n_small)
    y_small = _ffn(h2_small, wg, wu, wd, x1_small, fw, tm=n_small, tf=512)

    def with_meta(main, small):
        meta = jnp.broadcast_to(small[n_small:][None], (b, N_META, main.shape[-1]))
        return jnp.concatenate([meta, main.reshape(b, s, -1)], axis=1)

    k_prompt = with_meta(kf, kf_s).reshape(1, b, N_META + s, DIFF_HEADS, 2, DIFF_HD)
    v_prompt = with_meta(vf, vf_s).reshape(1, b, N_META + s, DIFF_HEADS, DIFF_DV)
    gla_prompt = st_main.swapaxes(-1, -2)[None].astype(state_gla.dtype)
    k_sample = kf_s[:n_small].reshape(1, bd, td, DIFF_HEADS, 2, DIFF_HD)
    v_sample = vf_s[:n_small].reshape(1, bd, td, DIFF_HEADS, DIFF_DV)
    gla_sample = st_small[:bd].swapaxes(-1, -2)[None].astype(state_gla.dtype)
    return (y_main.reshape(b, s, d), y_small.reshape(bd, td, d), k_prompt, v_prompt, gla_prompt,
            k_sample, v_sample, gla_sample)
```

```python
import functools
import math

import jax
import jax.numpy as jnp
from jax import lax
from jax.experimental import pallas as pl
from jax.experimental.pallas import tpu as pltpu

F32 = jnp.float32
BF16 = jnp.bfloat16

N_META = 16
GLA_HEADS = 4
GLA_DK = 128
GLA_DV = 256
GLA_GATE_RANK = 16
GLA_GATE_NORM = 16.0
GLA_CHUNK = 64
DIFF_HEADS = 4
DIFF_HD = 128
DIFF_DV = 256
ROT_DIM = DIFF_HD // 4
ROPE_THETA = 500000.0
PAGE_SIZE = 128
EPS = 1e-6
NEG_BIG = -1e30
LOG2E = math.log2(math.e)

LANES = 128
TOKEN_ROWS = 2 * DIFF_HEADS
VMEM_LIMIT = 56 * 1024 * 1024


def _dot(a, b):
    return jnp.dot(a, b, preferred_element_type=F32)


def _dot_nt(a, b):
    return lax.dot_general(a, b, (((1,), (1,)), ((), ())), preferred_element_type=F32)


def _dot_tn(a, b):
    return lax.dot_general(a, b, (((0,), (0,)), ((), ())), preferred_element_type=F32)


def _rms(x):
    return lax.rsqrt(jnp.mean(x * x, axis=-1, keepdims=True) + EPS)


def _silu(x):
    return x * jax.nn.sigmoid(x)


def _lambda(lq1, lk1, lq2, lk2, lam_init):
    return (jnp.exp(jnp.sum(lq1[...] * lk1[...], axis=-1, keepdims=True))
            - jnp.exp(jnp.sum(lq2[...] * lk2[...], axis=-1, keepdims=True)) + lam_init)


def _lane_fold(p):
    acc = p[:, :LANES]
    for i in range(1, p.shape[1] // LANES):
        acc = acc + p[:, i * LANES:(i + 1) * LANES]
    return acc


def _proj_kernel(x_ref, n1_ref, w_ref, wa_ref, wa2_ref, ba_ref, c_ref, s1_ref, s2_ref,
                 qg_ref, kg_ref, vg_ref, gk_ref, go_ref, qd_ref, kd_ref, vd_ref, kf_ref, vf_ref):
    x = x_ref[...]
    tm = x.shape[0]
    h = (x * _rms(x) * n1_ref[...]).astype(BF16)
    cw = 512

    def mm(c):
        return _dot(h, w_ref[:, c * cw:(c + 1) * cw])

    qg_ref[...] = (mm(0) * (GLA_DK ** -0.5)).astype(BF16)
    kg_ref[...] = mm(1).astype(BF16)
    for c in range(2):
        vg_ref[:, c * cw:(c + 1) * cw] = mm(2 + c).astype(BF16)
        go_ref[:, c * cw:(c + 1) * cw] = mm(4 + c).astype(BF16)

    a_lr = _dot(h, wa_ref[...]).astype(BF16)
    z = _dot(a_lr, wa2_ref[...]) + ba_ref[...]
    gk_ref[...] = (jnp.minimum(z, 0.0) - jnp.log(1.0 + jnp.exp(-jnp.abs(z)))) * (1.0 / GLA_GATE_NORM)

    reps = cw // LANES
    cos = jnp.tile(c_ref[...], (1, reps))
    sin_lo = jnp.tile(s1_ref[...], (1, reps))
    sin_hi = jnp.tile(s2_ref[...], (1, reps))

    def rope(v):
        return (v * cos + pltpu.roll(v, ROT_DIM // 2, 1) * sin_lo
                + pltpu.roll(v, cw - ROT_DIM // 2, 1) * sin_hi)

    sub_per_chunk = cw // DIFF_HD
    heads_per_chunk = cw // DIFF_DV
    for c in range(2):
        cols = slice(c * cw, (c + 1) * cw)
        qd_ref[:, cols] = (rope(mm(6 + c)) * (DIFF_HD ** -0.5 * LOG2E)).astype(BF16)
        kr = rope(mm(8 + c))
        kd_ref[:, cols] = kr.astype(BF16)
        for i in range(sub_per_chunk):
            kf_ref[pl.ds(c * sub_per_chunk + i, tm, stride=TOKEN_ROWS), :] = kr[:, i * LANES:(i + 1) * LANES]
        vv = mm(10 + c)
        vd_ref[:, cols] = vv.astype(BF16)
        for i in range(heads_per_chunk):
            for half in range(DIFF_DV // LANES):
                lo = i * DIFF_DV + half * LANES
                vf_ref[pl.ds(half * DIFF_HEADS + c * heads_per_chunk + i, tm, stride=TOKEN_ROWS), :] = (
                    vv[:, lo:lo + LANES])


def _proj(x, n1, w, wa, wa2, ba, cos, s1, s2, tm, n_seq, kv_rows, kv_spec):
    r, d = x.shape
    ncol = w.shape[1]
    nb = r // n_seq // tm
    row = lambda width: pl.BlockSpec((tm, width), lambda b, i: (b * nb + i, 0))
    tab = pl.BlockSpec((tm, LANES), lambda b, i: (i, 0))
    full = lambda a: pl.BlockSpec(a.shape, lambda b, i: (0, 0))
    widths = (512, 512, 1024, 512, 1024, 1024, 1024, 1024)
    dtypes = (BF16, BF16, BF16, F32, BF16, BF16, BF16, BF16)
    kv_shape = jax.ShapeDtypeStruct((kv_rows, LANES), F32)
    return pl.pallas_call(
        _proj_kernel,
        grid=(n_seq, nb),
        in_specs=[row(d), full(n1),
                  pl.BlockSpec((d, ncol), lambda b, i: (0, 0), pipeline_mode=pl.Buffered(1)),
                  full(wa), full(wa2), full(ba), tab, tab, tab],
        out_specs=[row(wd) for wd in widths] + [kv_spec, kv_spec],
        out_shape=[jax.ShapeDtypeStruct((r, wd), dt) for wd, dt in zip(widths, dtypes)]
        + [kv_shape, kv_shape],
        compiler_params=pltpu.CompilerParams(dimension_semantics=("parallel", "parallel"),
                                             vmem_limit_bytes=VMEM_LIMIT),
        name="proj",
    )(x, n1, w, wa, wa2, ba, cos, s1, s2)


def _gla_kernel(q_ref, k_ref, v_ref, gk_ref, go_ref, s0_ref, nw_ref, o_ref, st_ref, s_scr,
                *, chunk, n_chunks):
    t = pl.program_id(1)

    @pl.when(t == 0)
    def _():
        s_scr[...] = s0_ref[...]

    c = chunk
    row = lax.broadcasted_iota(jnp.int32, (c, c), 0)
    col = lax.broadcasted_iota(jnp.int32, (c, c), 1)
    tril = col <= row
    tri = jnp.where(tril, 1.0, 0.0).astype(BF16)

    def chunk_body(ci, carry):
        rows = pl.ds(pl.multiple_of(ci * c, c), c)
        g = gk_ref[rows, :]
        g_hi = g.astype(BF16)
        g_lo = (g - g_hi.astype(F32)).astype(BF16)
        bcum = _dot(tri, g_hi) + _dot(tri, g_lo)
        b_mid = bcum[c // 2:c // 2 + 1, :]
        b_last = bcum[c - 1:c, :]
        q = q_ref[rows, :].astype(F32)
        k = k_ref[rows, :].astype(F32)
        qe = (q * jnp.exp(bcum - b_mid)).astype(BF16)
        ke = (k * jnp.exp(b_mid - bcum)).astype(BF16)
        qs = (q * jnp.exp(bcum)).astype(BF16)
        kl = (k * jnp.exp(b_last - bcum)).astype(BF16)
        decay = jnp.exp(b_last)
        for h in range(GLA_HEADS):
            hk = slice(h * GLA_DK, (h + 1) * GLA_DK)
            hv = slice(h * GLA_DV, (h + 1) * GLA_DV)
            vh = v_ref[rows, hv]
            a = jnp.where(tril, _dot_nt(qe[:, hk], ke[:, hk]), 0.0)
            st = s_scr[h]
            o = _dot(a.astype(BF16), vh) + _dot_nt(qs[:, hk], st.astype(BF16))
            s_scr[h] = decay[:, hk] * st + _dot_tn(vh, kl[:, hk])
            gate = go_ref[rows, hv].astype(F32)
            o_ref[rows, hv] = (o * _rms(o) * nw_ref[...] * _silu(gate)).astype(BF16)
        return carry

    lax.fori_loop(0, n_chunks, chunk_body, 0)

    @pl.when(t == pl.num_programs(1) - 1)
    def _():
        st_ref[...] = s_scr[...]


def _gla(q, k, v, gk, go, s0t, nw, chunk, n_chunks):
    b, t, _ = q.shape
    tc = chunk * n_chunks
    seq = lambda width: pl.BlockSpec((None, tc, width), lambda i, j: (i, j, 0))
    state = pl.BlockSpec((None, GLA_HEADS, GLA_DV, GLA_DK), lambda i, j: (i, 0, 0, 0))
    return pl.pallas_call(
        functools.partial(_gla_kernel, chunk=chunk, n_chunks=n_chunks),
        grid=(b, t // tc),
        in_specs=[seq(512), seq(512), seq(1024), seq(512), seq(1024), state,
                  pl.BlockSpec(nw.shape, lambda i, j: (0, 0))],
        out_specs=[seq(1024), state],
        out_shape=[jax.ShapeDtypeStruct((b, t, 1024), BF16),
                   jax.ShapeDtypeStruct((b, GLA_HEADS, GLA_DV, GLA_DK), F32)],
        scratch_shapes=[pltpu.VMEM((GLA_HEADS, GLA_DV, GLA_DK), F32)],
        compiler_params=pltpu.CompilerParams(dimension_semantics=("parallel", "arbitrary"),
                                             vmem_limit_bytes=VMEM_LIMIT),
        name="gla",
    )(q, k, v, gk, go, s0t, nw)


def _diff_prompt_kernel(q_ref, k_ref, v_ref, km_ref, vm_ref, lq1, lk1, lq2, lk2, nw_ref, o_ref,
                        m_scr, l_scr, acc_scr, *, tq, sub, lam_init):
    qi = pl.program_id(2)
    n_sub = tq // sub

    for c in range(2):
        cs = slice(c * DIFF_HD, (c + 1) * DIFF_HD)
        s = _dot_nt(q_ref[:, cs], km_ref[:, cs])
        m_new = jnp.max(s, axis=-1, keepdims=True)
        p = jnp.exp2(s - m_new)
        m_scr[c] = jnp.broadcast_to(m_new, (tq, LANES))
        l_scr[c] = jnp.broadcast_to(jnp.sum(p, axis=-1, keepdims=True) * (1.0 / LANES), (tq, LANES))
        acc_scr[c] = _dot(p.astype(BF16), vm_ref[...])

    def update(c, rs, s, vb):
        m_prev = m_scr[c, rs, :]
        m_new = jnp.maximum(m_prev, jnp.max(s, axis=-1, keepdims=True))
        alpha = jnp.exp2(m_prev - m_new)
        p = jnp.exp2(s - jnp.tile(m_new, (1, s.shape[1] // LANES)))
        l_scr[c, rs, :] = alpha * l_scr[c, rs, :] + _lane_fold(p)
        acc_scr[c, rs, :] = (jnp.tile(alpha, (1, DIFF_DV // LANES)) * acc_scr[c, rs, :]
                             + _dot(p.astype(BF16), vb))
        m_scr[c, rs, :] = m_new

    def body(j, carry):
        rows = pl.ds(pl.multiple_of(j * tq, tq), tq)
        vb = v_ref[rows, :]
        for c in range(2):
            cs = slice(c * DIFF_HD, (c + 1) * DIFF_HD)
            kc = k_ref[rows, cs]
            for r in range(n_sub):
                rs = slice(r * sub, (r + 1) * sub)
                update(c, rs, _dot_nt(q_ref[rs, cs], kc), vb)
        return carry

    lax.fori_loop(0, qi, body, 0)

    base = pl.multiple_of(qi * tq, tq)
    for c in range(2):
        cs = slice(c * DIFF_HD, (c + 1) * DIFF_HD)
        for r in range(n_sub):
            rs = slice(r * sub, (r + 1) * sub)
            nk = (r + 1) * sub
            rows = pl.ds(base, nk)
            s = _dot_nt(q_ref[rs, cs], k_ref[rows, cs])
            q_pos = lax.broadcasted_iota(jnp.int32, (sub, nk), 0) + r * sub
            k_pos = lax.broadcasted_iota(jnp.int32, (sub, nk), 1)
            update(c, rs, jnp.where(k_pos <= q_pos, s, NEG_BIG), v_ref[rows, :])

    lam = _lambda(lq1, lk1, lq2, lk2, lam_init)
    a1 = acc_scr[0] / jnp.sum(l_scr[0], axis=-1, keepdims=True)
    a2 = acc_scr[1] / jnp.sum(l_scr[1], axis=-1, keepdims=True)
    o = a1 - lam * a2
    o_ref[...] = (o * _rms(o) * nw_ref[...] * (1.0 - lam_init)).astype(BF16)


def _diff_prompt(q, k, v, km, vm, lams, nw, lam_init, tq, sub):
    b, s, _ = q.shape
    hw = 2 * DIFF_HD
    small = lambda a: pl.BlockSpec(a.shape, lambda i, h, j: (0, 0))
    return pl.pallas_call(
        functools.partial(_diff_prompt_kernel, tq=tq, sub=sub, lam_init=lam_init),
        grid=(b, DIFF_HEADS, s // tq),
        in_specs=[pl.BlockSpec((None, tq, hw), lambda i, h, j: (i, j, h)),
                  pl.BlockSpec((None, s, hw), lambda i, h, j: (i, 0, h)),
                  pl.BlockSpec((None, s, DIFF_DV), lambda i, h, j: (i, 0, h)),
                  pl.BlockSpec((N_META, hw), lambda i, h, j: (0, h)),
                  pl.BlockSpec((N_META, DIFF_DV), lambda i, h, j: (0, h)),
                  small(lams[0]), small(lams[1]), small(lams[2]), small(lams[3]), small(nw)],
        out_specs=pl.BlockSpec((None, tq, DIFF_DV), lambda i, h, j: (i, j, h)),
        out_shape=jax.ShapeDtypeStruct((b, s, DIFF_HEADS * DIFF_DV), BF16),
        scratch_shapes=[pltpu.VMEM((2, tq, LANES), F32), pltpu.VMEM((2, tq, LANES), F32),
                        pltpu.VMEM((2, tq, DIFF_DV), F32)],
        compiler_params=pltpu.CompilerParams(
            dimension_semantics=("parallel", "parallel", "arbitrary"),
            vmem_limit_bytes=VMEM_LIMIT),
        name="diff_prompt",
    )(q, k, v, km, vm, *lams, nw)


Q_ROWS = 8
NEW_PAD = 16


def _diff_sample_kernel(pt_ref, q_ref, *refs, group, n_new, lam_init):
    k_refs = refs[:group]
    v_refs = refs[group:2 * group]
    kn_ref, vn_ref, lq1, lk1, lq2, lk2, nw_ref, o_ref, m_scr, l_scr, acc_scr = refs[2 * group:]
    g = pl.program_id(1)

    @pl.when(g == 0)
    def _():
        m_scr[...] = jnp.full(m_scr.shape, NEG_BIG, F32)
        l_scr[...] = jnp.zeros(l_scr.shape, F32)
        acc_scr[...] = jnp.zeros(acc_scr.shape, F32)

    q = q_ref[...]

    def update(s, vb):
        m_prev = m_scr[...][:, :1]
        m_new = jnp.maximum(m_prev, jnp.max(s, axis=-1, keepdims=True))
        alpha = jnp.exp2(m_prev - m_new)
        p = jnp.exp2(s - m_new)
        l_scr[...] = alpha * l_scr[...] + jnp.sum(p, axis=-1, keepdims=True)
        acc_scr[...] = alpha * acc_scr[...] + _dot(p.astype(BF16), vb)
        m_scr[...] = jnp.broadcast_to(m_new, m_scr.shape)

    def tile_row(ref, r):
        return ref[pl.ds(r, PAGE_SIZE, stride=TOKEN_ROWS), :].astype(BF16)

    def k_page(ref):
        return jnp.concatenate([tile_row(ref, hc) for hc in range(TOKEN_ROWS)], axis=1)

    def v_page(ref):
        return jnp.concatenate([tile_row(ref, half * DIFF_HEADS + h) for h in range(DIFF_HEADS)
                                for half in range(DIFF_DV // LANES)], axis=1)

    kg = jnp.concatenate([k_page(r) for r in k_refs], axis=0)
    vg = jnp.concatenate([v_page(r) for r in v_refs], axis=0)
    update(_dot_nt(q, kg), vg)

    @pl.when(g == pl.num_programs(1) - 1)
    def _():
        s = _dot_nt(q, kn_ref[...])
        t = lax.broadcasted_iota(jnp.int32, s.shape, 0) % Q_ROWS
        j = lax.broadcasted_iota(jnp.int32, s.shape, 1)
        s = jnp.where(j <= jnp.minimum(t, n_new - 1), s, NEG_BIG)
        update(s, vn_ref[...])
        lam = _lambda(lq1, lk1, lq2, lk2, lam_init)
        out = acc_scr[...] / l_scr[...][:, :1]
        for h in range(DIFF_HEADS):
            r0 = h * 2 * Q_ROWS
            cols = slice(h * DIFF_DV, (h + 1) * DIFF_DV)
            o = out[r0:r0 + Q_ROWS, cols] - lam * out[r0 + Q_ROWS:r0 + 2 * Q_ROWS, cols]
            o_ref[:, cols] = o * _rms(o) * nw_ref[...] * (1.0 - lam_init)


def _diff_sample(page_table, qbd, cache_k, cache_v, k_new, v_new, lams, nw, lam_init, n_new, group):
    bd, n_pages = page_table.shape
    nrow = qbd.shape[1]
    vw = DIFF_HEADS * DIFF_DV
    n_groups = n_pages // group
    page_rows = PAGE_SIZE * TOKEN_ROWS

    def page_map(b, g, pt, i):
        return (pt[b, g * group + i], 0)

    per_b = lambda a: pl.BlockSpec((None,) + a.shape[1:], lambda b, g, pt: (b, 0, 0))
    small = lambda a: pl.BlockSpec(a.shape, lambda b, g, pt: (0, 0))
    page_specs = [pl.BlockSpec((page_rows, LANES), functools.partial(page_map, i=i))
                  for i in range(group)]
    in_specs = ([per_b(qbd)] + page_specs + page_specs
                + [per_b(k_new), per_b(v_new)] + [small(a) for a in lams] + [small(nw)])
    return pl.pallas_call(
        functools.partial(_diff_sample_kernel, group=group, n_new=n_new, lam_init=lam_init),
        grid_spec=pltpu.PrefetchScalarGridSpec(
            num_scalar_prefetch=1,
            grid=(bd, n_groups),
            in_specs=in_specs,
            out_specs=pl.BlockSpec((None, Q_ROWS, vw), lambda b, g, pt: (b, 0, 0)),
            scratch_shapes=[pltpu.VMEM((nrow, LANES), F32), pltpu.VMEM((nrow, LANES), F32),
                            pltpu.VMEM((nrow, vw), F32)]),
        out_shape=jax.ShapeDtypeStruct((bd, Q_ROWS, vw), F32),
        compiler_params=pltpu.CompilerParams(dimension_semantics=("parallel", "arbitrary"),
                                             vmem_limit_bytes=VMEM_LIMIT),
        name="diff_sample",
    )(page_table, qbd, *([cache_k] * group), *([cache_v] * group), k_new, v_new, *lams, nw)


def _out_proj_kernel(x_ref, mg_ref, md_ref, wo_ref, n2_ref, x1_ref, h2_ref):
    half = mg_ref.shape[1]
    y = x_ref[...] + _dot(mg_ref[...], wo_ref[:half, :]) + _dot(md_ref[...], wo_ref[half:, :])
    x1_ref[...] = y
    h2_ref[...] = (y * _rms(y) * n2_ref[...]).astype(BF16)


def _out_proj(x, mg, md, wo, n2, tm):
    r, d = x.shape
    row = lambda a: pl.BlockSpec((tm, a.shape[1]), lambda i: (i, 0))
    full = lambda a: pl.BlockSpec(a.shape, lambda i: (0, 0))
    return pl.pallas_call(
        _out_proj_kernel,
        grid=(r // tm,),
        in_specs=[row(x), row(mg), row(md), full(wo), full(n2)],
        out_specs=[row(x), row(x)],
        out_shape=[jax.ShapeDtypeStruct((r, d), F32), jax.ShapeDtypeStruct((r, d), BF16)],
        compiler_params=pltpu.CompilerParams(dimension_semantics=("parallel",),
                                             vmem_limit_bytes=VMEM_LIMIT),
        name="out_proj",
    )(x, mg, md, wo, n2)


def _ffn_kernel(h_ref, wg_ref, wu_ref, wd_ref, x_ref, fw_ref, y_ref, acc_scr):
    j = pl.program_id(1)

    @pl.when(j == 0)
    def _():
        acc_scr[...] = jnp.zeros(acc_scr.shape, F32)

    h = h_ref[...]
    act = (_silu(_dot(h, wg_ref[...])) * _dot(h, wu_ref[...])).astype(BF16)
    acc_scr[...] += _dot(act, wd_ref[...])

    @pl.when(j == pl.num_programs(1) - 1)
    def _():
        y = x_ref[...] + acc_scr[...]
        y_ref[...] = y * _rms(y) * fw_ref[...]


def _ffn(h2, wg, wu, wd, x1, fw, tm, tf):
    r, d = h2.shape
    f = wg.shape[1]
    return pl.pallas_call(
        _ffn_kernel,
        grid=(r // tm, f // tf),
        in_specs=[pl.BlockSpec((tm, d), lambda i, j: (i, 0)),
                  pl.BlockSpec((d, tf), lambda i, j: (0, j)),
                  pl.BlockSpec((d, tf), lambda i, j: (0, j)),
                  pl.BlockSpec((tf, d), lambda i, j: (j, 0)),
                  pl.BlockSpec((tm, d), lambda i, j: (i, 0)),
                  pl.BlockSpec(fw.shape, lambda i, j: (0, 0))],
        out_specs=pl.BlockSpec((tm, d), lambda i, j: (i, 0)),
        out_shape=jax.ShapeDtypeStruct((r, d), F32),
        scratch_shapes=[pltpu.VMEM((tm, d), F32)],
        compiler_params=pltpu.CompilerParams(dimension_semantics=("parallel", "arbitrary"),
                                             vmem_limit_bytes=VMEM_LIMIT),
        name="ffn",
    )(h2, wg, wu, wd, x1, fw)


def _rope_tables(pos):
    half = ROT_DIM // 2
    inv = ROPE_THETA ** (-jnp.arange(half, dtype=F32) * 2.0 / ROT_DIM)
    ang = pos.astype(F32)[:, None] * inv[None, :]
    cos, sin = jnp.cos(ang), jnp.sin(ang)
    n = pos.shape[0]
    zeros = lambda w: jnp.zeros((n, w), F32)
    c = jnp.concatenate([cos, cos, jnp.ones((n, LANES - ROT_DIM), F32)], axis=1)
    s_lo = jnp.concatenate([zeros(half), sin, zeros(LANES - ROT_DIM)], axis=1)
    s_hi = jnp.concatenate([-sin, zeros(LANES - half)], axis=1)
    return c, s_lo, s_hi


def _value_rows_to_heads(v2d, n_tok):
    halves = DIFF_DV // LANES
    v = v2d.reshape(n_tok, halves, DIFF_HEADS, LANES).transpose(0, 2, 1, 3)
    return v.reshape(n_tok, DIFF_HEADS, DIFF_DV)


def kernel(x_prompt, x_sample, cache_k, cache_v, state_gla, page_table, meta_tokens, norm1_w, w_in,
           w_a2, b_a, gla_norm_w, lambda_q1, lambda_k1, lambda_q2, lambda_k2, diff_norm_w, w_o,
           norm2_w, w_gate, w_up, w_down, final_norm_w):
    b, s, d = x_prompt.shape
    bd, td, _ = x_sample.shape
    assert norm1_w.shape[0] == 1, "single-layer step only"
    assert td <= Q_ROWS and td <= N_META
    n_phys = cache_k.shape[1]
    past_len = page_table.shape[1] * PAGE_SIZE
    lam_init = 0.8 - 0.6 * math.exp(-0.3 * 0)
    n_small = bd * td
    n_tok = N_META + s

    w = w_in[0]
    g0 = 2 * GLA_HEADS * GLA_DK + GLA_HEADS * GLA_DV
    w_main = jnp.concatenate([w[:, :g0], w[:, g0 + GLA_GATE_RANK:]], axis=1).astype(BF16)
    w_a1 = jnp.pad(w[:, g0:g0 + GLA_GATE_RANK], ((0, 0), (0, LANES - GLA_GATE_RANK))).astype(BF16)
    w_a2p = jnp.pad(w_a2[0], ((0, LANES - GLA_GATE_RANK), (0, 0))).astype(BF16)
    row2 = lambda a: a.reshape(1, -1).astype(F32)
    n1, ba, gnw, dnw, n2, fw = (row2(norm1_w[0]), row2(b_a[0]), row2(gla_norm_w[0]),
                                row2(diff_norm_w[0]), row2(norm2_w[0]), row2(final_norm_w))
    lams = tuple(row2(a[0]) for a in (lambda_q1, lambda_k1, lambda_q2, lambda_k2))
    wo, wg, wu, wd = (w_o[0].astype(BF16), w_gate[0].astype(BF16), w_up[0].astype(BF16),
                      w_down[0].astype(BF16))

    x_main = x_prompt.reshape(b * s, d)
    x_new = x_sample.reshape(n_small, d)
    x_small = jnp.concatenate([x_new, meta_tokens.astype(F32)], axis=0)
    pos_small = jnp.concatenate([past_len + jnp.arange(n_small) % td, jnp.arange(N_META)])

    tm = 256
    kv_main = pl.BlockSpec((pl.Element(tm * TOKEN_ROWS), pl.Element(LANES)),
                           lambda i, j: ((i * n_tok + N_META + j * tm) * TOKEN_ROWS, 0))
    proj_main = _proj(x_main, n1, w_main, w_a1, w_a2p, ba, *_rope_tables(N_META + jnp.arange(s)),
                      tm=tm, n_seq=b, kv_rows=b * n_tok * TOKEN_ROWS, kv_spec=kv_main)
    r_small = n_small + N_META
    kv_small = pl.BlockSpec((r_small * TOKEN_ROWS, LANES), lambda i, j: (0, 0))
    proj_small = _proj(x_small, n1, w_main, w_a1, w_a2p, ba, *_rope_tables(pos_small),
                       tm=r_small, n_seq=1, kv_rows=r_small * TOKEN_ROWS, kv_spec=kv_small)
    qg, kg, vg, gk, go, qd, kd, vd, kf, vf = proj_main
    qg_s, kg_s, vg_s, gk_s, go_s, qd_s, kd_s, vd_s, kf_s, vf_s = proj_small

    def small_seq(a):
        new = jnp.pad(a[:n_small].reshape(bd, td, -1), ((0, 0), (0, N_META - td), (0, 0)))
        return jnp.concatenate([new, a[n_small:][None]], axis=0)

    s0_small = jnp.concatenate([state_gla[0].astype(F32),
                                jnp.zeros((1,) + state_gla.shape[2:], F32)], axis=0)
    mg_small, st_small = _gla(small_seq(qg_s), small_seq(kg_s), small_seq(vg_s), small_seq(gk_s),
                              small_seq(go_s), s0_small.swapaxes(-1, -2), gnw,
                              chunk=N_META, n_chunks=1)
    seq = lambda a: a.reshape(b, s, -1)
    s0_main = jnp.broadcast_to(st_small[bd:], (b,) + st_small.shape[1:])
    mg_main, st_main = _gla(seq(qg), seq(kg), seq(vg), seq(gk), seq(go), s0_main, gnw,
                            chunk=GLA_CHUNK, n_chunks=8)

    md_main = _diff_prompt(seq(qd), seq(kd), seq(vd), kd_s[n_small:], vd_s[n_small:], lams, dnw,
                           lam_init, tq=1024, sub=128)

    n_sub = 2 * DIFF_HEADS
    q4 = jnp.pad(qd_s[:n_small].reshape(bd, td, n_sub, DIFF_HD),
                 ((0, 0), (0, Q_ROWS - td), (0, 0), (0, 0))).transpose(0, 2, 1, 3)
    qbd = (q4[:, :, :, None, :] * jnp.eye(n_sub, dtype=BF16)[None, :, None, :, None])
    qbd = qbd.reshape(bd, n_sub * Q_ROWS, n_sub * DIFF_HD)
    pad_new = lambda a: jnp.pad(a[:n_small].reshape(bd, td, -1), ((0, 0), (0, NEW_PAD - td), (0, 0)))
    halves = DIFF_DV // LANES
    cache_k2 = cache_k[0].reshape(n_phys * PAGE_SIZE * TOKEN_ROWS, LANES)
    cache_v2 = (cache_v[0].reshape(n_phys * PAGE_SIZE, DIFF_HEADS, halves, LANES).transpose(0, 2, 1, 3)
                .reshape(n_phys * PAGE_SIZE * TOKEN_ROWS, LANES))
    od_small = _diff_sample(page_table, qbd, cache_k2, cache_v2, pad_new(kd_s), pad_new(vd_s), lams,
                            dnw, lam_init, n_new=td, group=8)
    md_small = od_small[:, :td].reshape(n_small, -1).astype(BF16)

    x1_main, h2_main = _out_proj(x_main, mg_main.reshape(b * s, -1), md_main.reshape(b * s, -1),
                                 wo, n2, tm=256)
    y_main = _ffn(h2_main, wg, wu, wd, x1_main, fw, tm=512, tf=512)
    x1_small, h2_small = _out_proj(x_new, mg_small[:bd, :td].reshape(n_small, -1), md_small, wo, n2,
                                   tm=n_small)
    y_small = _ffn(h2_small, wg, wu, wd, x1_small, fw, tm=n_small, tf=512)

    meta_rows = slice(n_small * TOKEN_ROWS, r_small * TOKEN_ROWS)
    for i in range(b):
        at = (i * n_tok * TOKEN_ROWS, 0)
        kf = lax.dynamic_update_slice(kf, kf_s[meta_rows], at)
        vf = lax.dynamic_update_slice(vf, vf_s[meta_rows], at)
    k_prompt = kf.reshape(1, b, n_tok, DIFF_HEADS, 2, DIFF_HD)
    v_prompt = _value_rows_to_heads(vf, b * n_tok).reshape(1, b, n_tok, DIFF_HEADS, DIFF_DV)
    gla_prompt = st_main.swapaxes(-1, -2)[None].astype(state_gla.dtype)
    k_sample = kf_s[:n_small * TOKEN_ROWS].reshape(1, bd, td, DIFF_HEADS, 2, DIFF_HD)
    v_sample = _value_rows_to_heads(vf_s[:n_small * TOKEN_ROWS], n_small).reshape(
        1, bd, td, DIFF_HEADS, DIFF_DV)
    gla_sample = st_small[:bd].swapaxes(-1, -2)[None].astype(state_gla.dtype)
    return (y_main.reshape(b, s, d), y_small.reshape(bd, td, d), k_prompt, v_prompt, gla_prompt,
            k_sample, v_sample, gla_sample)
```

```python
import functools
import math

import jax
import jax.numpy as jnp
from jax import lax
from jax.experimental import pallas as pl
from jax.experimental.pallas import tpu as pltpu

F32 = jnp.float32
BF16 = jnp.bfloat16

N_META = 16
GLA_HEADS = 4
GLA_DK = 128
GLA_DV = 256
GLA_GATE_RANK = 16
GLA_GATE_NORM = 16.0
GLA_CHUNK = 64
DIFF_HEADS = 4
DIFF_HD = 128
DIFF_DV = 256
ROT_DIM = DIFF_HD // 4
ROPE_THETA = 500000.0
PAGE_SIZE = 128
EPS = 1e-6
NEG_BIG = -1e30
LOG2E = math.log2(math.e)

LANES = 128
TOKEN_ROWS = 2 * DIFF_HEADS
VMEM_LIMIT = 56 * 1024 * 1024


def _dot(a, b):
    return jnp.dot(a, b, preferred_element_type=F32)


def _dot_nt(a, b):
    return lax.dot_general(a, b, (((1,), (1,)), ((), ())), preferred_element_type=F32)


def _dot_tn(a, b):
    return lax.dot_general(a, b, (((0,), (0,)), ((), ())), preferred_element_type=F32)


def _rms(x):
    return lax.rsqrt(jnp.mean(x * x, axis=-1, keepdims=True) + EPS)


def _silu(x):
    return x * jax.nn.sigmoid(x)


def _lambda(lq1, lk1, lq2, lk2, lam_init):
    return (jnp.exp(jnp.sum(lq1[...] * lk1[...], axis=-1, keepdims=True))
            - jnp.exp(jnp.sum(lq2[...] * lk2[...], axis=-1, keepdims=True)) + lam_init)


def _lane_fold(p):
    acc = p[:, :LANES]
    for i in range(1, p.shape[1] // LANES):
        acc = acc + p[:, i * LANES:(i + 1) * LANES]
    return acc


def _proj_kernel(x_ref, n1_ref, w_ref, wa_ref, wa2_ref, ba_ref, c_ref, s1_ref, s2_ref,
                 qg_ref, kg_ref, vg_ref, gk_ref, go_ref, qd_ref, kd_ref, vd_ref, kf_ref, vf_ref):
    x = x_ref[...]
    tm = x.shape[0]
    h = (x * _rms(x) * n1_ref[...]).astype(BF16)
    cw = 512

    def mm(c):
        return _dot(h, w_ref[:, c * cw:(c + 1) * cw])

    qg_ref[...] = (mm(0) * (GLA_DK ** -0.5)).astype(BF16)
    kg_ref[...] = mm(1).astype(BF16)
    for c in range(2):
        vg_ref[:, c * cw:(c + 1) * cw] = mm(2 + c).astype(BF16)
        go_ref[:, c * cw:(c + 1) * cw] = mm(4 + c).astype(BF16)

    a_lr = _dot(h, wa_ref[...]).astype(BF16)
    z = _dot(a_lr, wa2_ref[...]) + ba_ref[...]
    gk_ref[...] = (jnp.minimum(z, 0.0) - jnp.log(1.0 + jnp.exp(-jnp.abs(z)))) * (1.0 / GLA_GATE_NORM)

    reps = cw // LANES
    cos = jnp.tile(c_ref[...], (1, reps))
    sin_lo = jnp.tile(s1_ref[...], (1, reps))
    sin_hi = jnp.tile(s2_ref[...], (1, reps))

    def rope(v):
        return (v * cos + pltpu.roll(v, ROT_DIM // 2, 1) * sin_lo
                + pltpu.roll(v, cw - ROT_DIM // 2, 1) * sin_hi)

    sub_per_chunk = cw // DIFF_HD
    heads_per_chunk = cw // DIFF_DV
    for c in range(2):
        cols = slice(c * cw, (c + 1) * cw)
        qd_ref[:, cols] = (rope(mm(6 + c)) * (DIFF_HD ** -0.5 * LOG2E)).astype(BF16)
        kr = rope(mm(8 + c))
        kd_ref[:, cols] = kr.astype(BF16)
        for i in range(sub_per_chunk):
            kf_ref[pl.ds(c * sub_per_chunk + i, tm, stride=TOKEN_ROWS), :] = kr[:, i * LANES:(i + 1) * LANES]
        vv = mm(10 + c)
        vd_ref[:, cols] = vv.astype(BF16)
        for i in range(heads_per_chunk):
            for half in range(DIFF_DV // LANES):
                lo = i * DIFF_DV + half * LANES
                vf_ref[pl.ds(half * DIFF_HEADS + c * heads_per_chunk + i, tm, stride=TOKEN_ROWS), :] = (
                    vv[:, lo:lo + LANES])


def _proj(x, n1, w, wa, wa2, ba, cos, s1, s2, tm, n_seq, kv_rows, kv_spec):
    r, d = x.shape
    ncol = w.shape[1]
    nb = r // n_seq // tm
    row = lambda width: pl.BlockSpec((tm, width), lambda b, i: (b * nb + i, 0))
    tab = pl.BlockSpec((tm, LANES), lambda b, i: (i, 0))
    full = lambda a: pl.BlockSpec(a.shape, lambda b, i: (0, 0))
    widths = (512, 512, 1024, 512, 1024, 1024, 1024, 1024)
    dtypes = (BF16, BF16, BF16, F32, BF16, BF16, BF16, BF16)
    kv_shape = jax.ShapeDtypeStruct((kv_rows, LANES), F32)
    return pl.pallas_call(
        _proj_kernel,
        grid=(n_seq, nb),
        in_specs=[row(d), full(n1),
                  pl.BlockSpec((d, ncol), lambda b, i: (0, 0), pipeline_mode=pl.Buffered(1)),
                  full(wa), full(wa2), full(ba), tab, tab, tab],
        out_specs=[row(wd) for wd in widths] + [kv_spec, kv_spec],
        out_shape=[jax.ShapeDtypeStruct((r, wd), dt) for wd, dt in zip(widths, dtypes)]
        + [kv_shape, kv_shape],
        compiler_params=pltpu.CompilerParams(dimension_semantics=("parallel", "parallel"),
                                             vmem_limit_bytes=VMEM_LIMIT),
        name="proj",
    )(x, n1, w, wa, wa2, ba, cos, s1, s2)


def _gla_kernel(q_ref, k_ref, v_ref, gk_ref, go_ref, s0_ref, nw_ref, o_ref, st_ref, s_scr,
                *, chunk, n_chunks, unroll):
    t = pl.program_id(1)
    n_seq = q_ref.shape[0]

    @pl.when(t == 0)
    def _():
        s_scr[...] = s0_ref[...]

    c = chunk
    row = lax.broadcasted_iota(jnp.int32, (c, c), 0)
    col = lax.broadcasted_iota(jnp.int32, (c, c), 1)
    tril = col <= row
    tri = jnp.where(tril, 1.0, 0.0).astype(BF16)

    def seq_chunk(si, rows):
        g = gk_ref[si, rows, :]
        g_hi = g.astype(BF16)
        g_lo = (g - g_hi.astype(F32)).astype(BF16)
        bcum = _dot(tri, g_hi) + _dot(tri, g_lo)
        b_mid = bcum[c // 2:c // 2 + 1, :]
        b_last = bcum[c - 1:c, :]
        q = q_ref[si, rows, :].astype(F32)
        k = k_ref[si, rows, :].astype(F32)
        qe = (q * jnp.exp(bcum - b_mid)).astype(BF16)
        ke = (k * jnp.exp(b_mid - bcum)).astype(BF16)
        qs = (q * jnp.exp(bcum)).astype(BF16)
        kl = (k * jnp.exp(b_last - bcum)).astype(BF16)
        decay = jnp.exp(b_last)
        for h in range(GLA_HEADS):
            hk = slice(h * GLA_DK, (h + 1) * GLA_DK)
            hv = slice(h * GLA_DV, (h + 1) * GLA_DV)
            vh = v_ref[si, rows, hv]
            a = jnp.where(tril, _dot_nt(qe[:, hk], ke[:, hk]), 0.0)
            st = s_scr[si, h]
            o = _dot(a.astype(BF16), vh) + _dot_nt(qs[:, hk], st.astype(BF16))
            s_scr[si, h] = decay[:, hk] * st + _dot_tn(vh, kl[:, hk])
            gate = go_ref[si, rows, hv].astype(F32)
            o_ref[si, rows, hv] = (o * _rms(o) * nw_ref[...] * _silu(gate)).astype(BF16)

    def chunk_body(ci, carry):
        rows = pl.ds(pl.multiple_of(ci * c, c), c)
        for si in range(n_seq):
            seq_chunk(si, rows)
        return carry

    lax.fori_loop(0, n_chunks, chunk_body, 0, unroll=unroll)

    @pl.when(t == pl.num_programs(1) - 1)
    def _():
        st_ref[...] = s_scr[...]


def _gla(q, k, v, gk, go, s0t, nw, chunk, n_chunks, n_seq, unroll=1):
    b, t, _ = q.shape
    tc = chunk * n_chunks
    seq = lambda width: pl.BlockSpec((n_seq, tc, width), lambda i, j: (i, j, 0))
    state = pl.BlockSpec((n_seq, GLA_HEADS, GLA_DV, GLA_DK), lambda i, j: (i, 0, 0, 0))
    return pl.pallas_call(
        functools.partial(_gla_kernel, chunk=chunk, n_chunks=n_chunks, unroll=unroll),
        grid=(b // n_seq, t // tc),
        in_specs=[seq(512), seq(512), seq(1024), seq(512), seq(1024), state,
                  pl.BlockSpec(nw.shape, lambda i, j: (0, 0))],
        out_specs=[seq(1024), state],
        out_shape=[jax.ShapeDtypeStruct((b, t, 1024), BF16),
                   jax.ShapeDtypeStruct((b, GLA_HEADS, GLA_DV, GLA_DK), F32)],
        scratch_shapes=[pltpu.VMEM((n_seq, GLA_HEADS, GLA_DV, GLA_DK), F32)],
        compiler_params=pltpu.CompilerParams(dimension_semantics=("parallel", "arbitrary"),
                                             vmem_limit_bytes=VMEM_LIMIT),
        name="gla",
    )(q, k, v, gk, go, s0t, nw)


def _diff_prompt_kernel(q_ref, k_ref, v_ref, km_ref, vm_ref, lq1, lk1, lq2, lk2, nw_ref, o_ref,
                        m_scr, l_scr, acc_scr, *, tq, sub, lam_init):
    qi = pl.program_id(2)
    n_sub = tq // sub

    for c in range(2):
        cs = slice(c * DIFF_HD, (c + 1) * DIFF_HD)
        s = _dot_nt(q_ref[:, cs], km_ref[:, cs])
        m_new = jnp.max(s, axis=-1, keepdims=True)
        p = jnp.exp2(s - m_new)
        m_scr[c] = jnp.broadcast_to(m_new, (tq, LANES))
        l_scr[c] = jnp.broadcast_to(jnp.sum(p, axis=-1, keepdims=True) * (1.0 / LANES), (tq, LANES))
        acc_scr[c] = _dot(p.astype(BF16), vm_ref[...])

    def update(c, rs, s, vb):
        m_prev = m_scr[c, rs, :]
        m_new = jnp.maximum(m_prev, jnp.max(s, axis=-1, keepdims=True))
        alpha = jnp.exp2(m_prev - m_new)
        p = jnp.exp2(s - jnp.tile(m_new, (1, s.shape[1] // LANES)))
        l_scr[c, rs, :] = alpha * l_scr[c, rs, :] + _lane_fold(p)
        acc_scr[c, rs, :] = (jnp.tile(alpha, (1, DIFF_DV // LANES)) * acc_scr[c, rs, :]
                             + _dot(p.astype(BF16), vb))
        m_scr[c, rs, :] = m_new

    def body(j, carry):
        rows = pl.ds(pl.multiple_of(j * tq, tq), tq)
        vb = v_ref[rows, :]
        for c in range(2):
            cs = slice(c * DIFF_HD, (c + 1) * DIFF_HD)
            kc = k_ref[rows, cs]
            for r in range(n_sub):
                rs = slice(r * sub, (r + 1) * sub)
                update(c, rs, _dot_nt(q_ref[rs, cs], kc), vb)
        return carry

    lax.fori_loop(0, qi, body, 0)

    base = pl.multiple_of(qi * tq, tq)
    for c in range(2):
        cs = slice(c * DIFF_HD, (c + 1) * DIFF_HD)
        for r in range(n_sub):
            rs = slice(r * sub, (r + 1) * sub)
            nk = (r + 1) * sub
            rows = pl.ds(base, nk)
            s = _dot_nt(q_ref[rs, cs], k_ref[rows, cs])
            q_pos = lax.broadcasted_iota(jnp.int32, (sub, nk), 0) + r * sub
            k_pos = lax.broadcasted_iota(jnp.int32, (sub, nk), 1)
            update(c, rs, jnp.where(k_pos <= q_pos, s, NEG_BIG), v_ref[rows, :])

    lam = _lambda(lq1, lk1, lq2, lk2, lam_init)
    a1 = acc_scr[0] / jnp.sum(l_scr[0], axis=-1, keepdims=True)
    a2 = acc_scr[1] / jnp.sum(l_scr[1], axis=-1, keepdims=True)
    o = a1 - lam * a2
    o_ref[...] = (o * _rms(o) * nw_ref[...] * (1.0 - lam_init)).astype(BF16)


def _diff_prompt(q, k, v, km, vm, lams, nw, lam_init, tq, sub):
    b, s, _ = q.shape
    hw = 2 * DIFF_HD
    small = lambda a: pl.BlockSpec(a.shape, lambda i, h, j: (0, 0))
    return pl.pallas_call(
        functools.partial(_diff_prompt_kernel, tq=tq, sub=sub, lam_init=lam_init),
        grid=(b, DIFF_HEADS, s // tq),
        in_specs=[pl.BlockSpec((None, tq, hw), lambda i, h, j: (i, j, h)),
                  pl.BlockSpec((None, s, hw), lambda i, h, j: (i, 0, h)),
                  pl.BlockSpec((None, s, DIFF_DV), lambda i, h, j: (i, 0, h)),
                  pl.BlockSpec((N_META, hw), lambda i, h, j: (0, h)),
                  pl.BlockSpec((N_META, DIFF_DV), lambda i, h, j: (0, h)),
                  small(lams[0]), small(lams[1]), small(lams[2]), small(lams[3]), small(nw)],
        out_specs=pl.BlockSpec((None, tq, DIFF_DV), lambda i, h, j: (i, j, h)),
        out_shape=jax.ShapeDtypeStruct((b, s, DIFF_HEADS * DIFF_DV), BF16),
        scratch_shapes=[pltpu.VMEM((2, tq, LANES), F32), pltpu.VMEM((2, tq, LANES), F32),
                        pltpu.VMEM((2, tq, DIFF_DV), F32)],
        compiler_params=pltpu.CompilerParams(
            dimension_semantics=("parallel", "parallel", "arbitrary"),
            vmem_limit_bytes=VMEM_LIMIT),
        name="diff_prompt",
    )(q, k, v, km, vm, *lams, nw)


Q_ROWS = 8
NEW_PAD = 16


def _decode_init(m_scr, l_scr, acc_scr):
    m_scr[...] = jnp.full(m_scr.shape, NEG_BIG, F32)
    l_scr[...] = jnp.zeros(l_scr.shape, F32)
    acc_scr[...] = jnp.zeros(acc_scr.shape, F32)


def _decode_update(m_scr, l_scr, acc_scr, s, vb):
    m_prev = m_scr[...][:, :1]
    m_new = jnp.maximum(m_prev, jnp.max(s, axis=-1, keepdims=True))
    alpha = jnp.exp2(m_prev - m_new)
    p = jnp.exp2(s - m_new)
    l_scr[...] = alpha * l_scr[...] + jnp.sum(p, axis=-1, keepdims=True)
    acc_scr[...] = alpha * acc_scr[...] + _dot(p.astype(BF16), vb)
    m_scr[...] = jnp.broadcast_to(m_new, m_scr.shape)


def _decode_pages(n_valid, q_ref, k_refs, v_refs, m_scr, l_scr, acc_scr):
    def tile_row(ref, r):
        return ref[pl.ds(r, PAGE_SIZE, stride=TOKEN_ROWS), :].astype(BF16)

    def k_page(ref):
        return jnp.concatenate([tile_row(ref, hc) for hc in range(TOKEN_ROWS)], axis=1)

    def v_page(ref):
        return jnp.concatenate([tile_row(ref, half * DIFF_HEADS + h) for h in range(DIFF_HEADS)
                                for half in range(DIFF_DV // LANES)], axis=1)

    kg = jnp.concatenate([k_page(r) for r in k_refs], axis=0)
    vg = jnp.concatenate([v_page(r) for r in v_refs], axis=0)
    s = _dot_nt(q_ref[...], kg)
    key = lax.broadcasted_iota(jnp.int32, s.shape, 1)
    _decode_update(m_scr, l_scr, acc_scr, jnp.where(key < n_valid * PAGE_SIZE, s, NEG_BIG), vg)


def _decode_finish(q_ref, kn_ref, vn_ref, lam_refs, nw_ref, o_ref, m_scr, l_scr, acc_scr,
                   *, n_new, lam_init):
    s = _dot_nt(q_ref[...], kn_ref[...])
    t = lax.broadcasted_iota(jnp.int32, s.shape, 0) % Q_ROWS
    j = lax.broadcasted_iota(jnp.int32, s.shape, 1)
    s = jnp.where(j <= jnp.minimum(t, n_new - 1), s, NEG_BIG)
    _decode_update(m_scr, l_scr, acc_scr, s, vn_ref[...])
    lam = _lambda(*lam_refs, lam_init)
    out = acc_scr[...] / l_scr[...][:, :1]
    for h in range(DIFF_HEADS):
        r0 = h * 2 * Q_ROWS
        cols = slice(h * DIFF_DV, (h + 1) * DIFF_DV)
        o = out[r0:r0 + Q_ROWS, cols] - lam * out[r0 + Q_ROWS:r0 + 2 * Q_ROWS, cols]
        o_ref[:, cols] = o * _rms(o) * nw_ref[...] * (1.0 - lam_init)


def _out_proj_kernel(x_ref, mg_ref, md_ref, wo_ref, n2_ref, x1_ref, h2_ref):
    half = mg_ref.shape[1]
    y = x_ref[...] + _dot(mg_ref[...], wo_ref[:half, :]) + _dot(md_ref[...], wo_ref[half:, :])
    x1_ref[...] = y
    h2_ref[...] = (y * _rms(y) * n2_ref[...]).astype(BF16)


def _out_proj(x, mg, md, wo, n2, tm):
    r, d = x.shape
    row = lambda a: pl.BlockSpec((tm, a.shape[1]), lambda i: (i, 0))
    full = lambda a: pl.BlockSpec(a.shape, lambda i: (0, 0))
    return pl.pallas_call(
        _out_proj_kernel,
        grid=(r // tm,),
        in_specs=[row(x), row(mg), row(md), full(wo), full(n2)],
        out_specs=[row(x), row(x)],
        out_shape=[jax.ShapeDtypeStruct((r, d), F32), jax.ShapeDtypeStruct((r, d), BF16)],
        compiler_params=pltpu.CompilerParams(dimension_semantics=("parallel",),
                                             vmem_limit_bytes=VMEM_LIMIT),
        name="out_proj",
    )(x, mg, md, wo, n2)


def _ffn_kernel(*refs, pages, steps_per_seq, n_pages, n_new, lam_init):
    if pages:
        refs = refs[1:]
    h_ref, wg_ref, wu_ref, wd_ref, x_ref, fw_ref = refs[:6]
    if pages:
        q_ref = refs[6]
        k_refs = refs[7:7 + pages]
        v_refs = refs[7 + pages:7 + 2 * pages]
        kn_ref, vn_ref, lq1, lk1, lq2, lk2, nw_ref, y_ref, od_ref, acc_scr, m_scr, l_scr, accd_scr = (
            refs[7 + 2 * pages:])
    else:
        y_ref, acc_scr = refs[6:]
    j = pl.program_id(1)
    n_j = pl.num_programs(1)
    if pages:
        g = (pl.program_id(0) * n_j + j) % steps_per_seq
        dec_scr = (m_scr, l_scr, accd_scr)

    @pl.when(j == 0)
    def _():
        acc_scr[...] = jnp.zeros(acc_scr.shape, F32)

    if pages:
        pl.when(g == 0)(functools.partial(_decode_init, *dec_scr))

    h = h_ref[...]
    act = (_silu(_dot(h, wg_ref[...])) * _dot(h, wu_ref[...])).astype(BF16)
    acc_scr[...] += _dot(act, wd_ref[...])
    if pages:
        _decode_pages(jnp.minimum(n_pages - g * pages, pages), q_ref, k_refs, v_refs, *dec_scr)

    @pl.when(j == n_j - 1)
    def _():
        y = x_ref[...] + acc_scr[...]
        y_ref[...] = y * _rms(y) * fw_ref[...]

    if pages:
        @pl.when(g == steps_per_seq - 1)
        def _():
            _decode_finish(q_ref, kn_ref, vn_ref, (lq1, lk1, lq2, lk2), nw_ref, od_ref, *dec_scr,
                           n_new=n_new, lam_init=lam_init)


def _ffn(h2, wg, wu, wd, x1, fw, tm, tf, decode=None):
    r, d = h2.shape
    f = wg.shape[1]
    n_i, n_j = r // tm, f // tf
    in_specs = [pl.BlockSpec((tm, d), lambda i, j, *_: (i, 0)),
                pl.BlockSpec((d, tf), lambda i, j, *_: (0, j)),
                pl.BlockSpec((d, tf), lambda i, j, *_: (0, j)),
                pl.BlockSpec((tf, d), lambda i, j, *_: (j, 0)),
                pl.BlockSpec((tm, d), lambda i, j, *_: (i, 0)),
                pl.BlockSpec(fw.shape, lambda i, j, *_: (0, 0))]
    out_specs = [pl.BlockSpec((tm, d), lambda i, j, *_: (i, 0))]
    out_shape = [jax.ShapeDtypeStruct((r, d), F32)]
    scratch = [pltpu.VMEM((tm, d), F32)]
    operands = [h2, wg, wu, wd, x1, fw]
    static = dict(pages=0, steps_per_seq=0, n_pages=0, n_new=0, lam_init=0.0)
    prefetch = []
    if decode is not None:
        page_table, qbd, cache_k, cache_v, k_new, v_new, lams, nw, lam_init, n_new = decode
        bd, n_pages = page_table.shape
        nrow = qbd.shape[1]
        vw = DIFF_HEADS * DIFF_DV
        steps_per_seq, rem = divmod(n_i * n_j, bd)
        assert rem == 0, "grid steps must split evenly over the decode sequences"
        pages = pl.cdiv(n_pages, steps_per_seq)
        assert (steps_per_seq - 1) * pages < n_pages and pages <= n_pages

        def seq_of(i, j):
            return (i * n_j + j) // steps_per_seq

        def page_map(i, j, pt, k):
            p = ((i * n_j + j) % steps_per_seq) * pages + k
            return (pt[seq_of(i, j), jnp.where(p < n_pages, p, p - pages)], 0)

        per_seq = lambda a: pl.BlockSpec((None,) + a.shape[1:], lambda i, j, pt: (seq_of(i, j), 0, 0))
        small = lambda a: pl.BlockSpec(a.shape, lambda i, j, pt: (0, 0))
        page_specs = [pl.BlockSpec((PAGE_SIZE * TOKEN_ROWS, LANES), functools.partial(page_map, k=k))
                      for k in range(pages)]
        in_specs += ([per_seq(qbd)] + page_specs + page_specs + [per_seq(k_new), per_seq(v_new)]
                     + [small(a) for a in lams] + [small(nw)])
        out_specs.append(pl.BlockSpec((None, Q_ROWS, vw), lambda i, j, pt: (seq_of(i, j), 0, 0)))
        out_shape.append(jax.ShapeDtypeStruct((bd, Q_ROWS, vw), F32))
        scratch += [pltpu.VMEM((nrow, LANES), F32), pltpu.VMEM((nrow, LANES), F32),
                    pltpu.VMEM((nrow, vw), F32)]
        operands += [qbd] + [cache_k] * pages + [cache_v] * pages + [k_new, v_new, *lams, nw]
        static = dict(pages=pages, steps_per_seq=steps_per_seq, n_pages=n_pages, n_new=n_new,
                      lam_init=lam_init)
        prefetch = [page_table]
    return pl.pallas_call(
        functools.partial(_ffn_kernel, **static),
        grid_spec=pltpu.PrefetchScalarGridSpec(
            num_scalar_prefetch=len(prefetch), grid=(n_i, n_j), in_specs=in_specs,
            out_specs=out_specs, scratch_shapes=scratch),
        out_shape=out_shape,
        compiler_params=pltpu.CompilerParams(dimension_semantics=("arbitrary", "arbitrary"),
                                             vmem_limit_bytes=VMEM_LIMIT),
        name="ffn_decode" if decode is not None else "ffn",
    )(*prefetch, *operands)


def _rope_tables(pos):
    half = ROT_DIM // 2
    inv = ROPE_THETA ** (-jnp.arange(half, dtype=F32) * 2.0 / ROT_DIM)
    ang = pos.astype(F32)[:, None] * inv[None, :]
    cos, sin = jnp.cos(ang), jnp.sin(ang)
    n = pos.shape[0]
    zeros = lambda w: jnp.zeros((n, w), F32)
    c = jnp.concatenate([cos, cos, jnp.ones((n, LANES - ROT_DIM), F32)], axis=1)
    s_lo = jnp.concatenate([zeros(half), sin, zeros(LANES - ROT_DIM)], axis=1)
    s_hi = jnp.concatenate([-sin, zeros(LANES - half)], axis=1)
    return c, s_lo, s_hi


def _value_rows_to_heads(v2d, n_tok):
    halves = DIFF_DV // LANES
    v = v2d.reshape(n_tok, halves, DIFF_HEADS, LANES).transpose(0, 2, 1, 3)
    return v.reshape(n_tok, DIFF_HEADS, DIFF_DV)


def kernel(x_prompt, x_sample, cache_k, cache_v, state_gla, page_table, meta_tokens, norm1_w, w_in,
           w_a2, b_a, gla_norm_w, lambda_q1, lambda_k1, lambda_q2, lambda_k2, diff_norm_w, w_o,
           norm2_w, w_gate, w_up, w_down, final_norm_w):
    b, s, d = x_prompt.shape
    bd, td, _ = x_sample.shape
    assert norm1_w.shape[0] == 1, "single-layer step only"
    assert td <= Q_ROWS and td <= N_META
    n_phys = cache_k.shape[1]
    past_len = page_table.shape[1] * PAGE_SIZE
    lam_init = 0.8 - 0.6 * math.exp(-0.3 * 0)
    n_small = bd * td
    n_tok = N_META + s

    w = w_in[0]
    g0 = 2 * GLA_HEADS * GLA_DK + GLA_HEADS * GLA_DV
    w_main = jnp.concatenate([w[:, :g0], w[:, g0 + GLA_GATE_RANK:]], axis=1).astype(BF16)
    w_a1 = jnp.pad(w[:, g0:g0 + GLA_GATE_RANK], ((0, 0), (0, LANES - GLA_GATE_RANK))).astype(BF16)
    w_a2p = jnp.pad(w_a2[0], ((0, LANES - GLA_GATE_RANK), (0, 0))).astype(BF16)
    row2 = lambda a: a.reshape(1, -1).astype(F32)
    n1, ba, gnw, dnw, n2, fw = (row2(norm1_w[0]), row2(b_a[0]), row2(gla_norm_w[0]),
                                row2(diff_norm_w[0]), row2(norm2_w[0]), row2(final_norm_w))
    lams = tuple(row2(a[0]) for a in (lambda_q1, lambda_k1, lambda_q2, lambda_k2))
    wo, wg, wu, wd = (w_o[0].astype(BF16), w_gate[0].astype(BF16), w_up[0].astype(BF16),
                      w_down[0].astype(BF16))

    x_main = x_prompt.reshape(b * s, d)
    x_new = x_sample.reshape(n_small, d)
    x_small = jnp.concatenate([x_new, meta_tokens.astype(F32)], axis=0)
    pos_small = jnp.concatenate([past_len + jnp.arange(n_small) % td, jnp.arange(N_META)])

    tm = 256
    kv_main = pl.BlockSpec((pl.Element(tm * TOKEN_ROWS), pl.Element(LANES)),
                           lambda i, j: ((i * n_tok + N_META + j * tm) * TOKEN_ROWS, 0))
    proj_main = _proj(x_main, n1, w_main, w_a1, w_a2p, ba, *_rope_tables(N_META + jnp.arange(s)),
                      tm=tm, n_seq=b, kv_rows=b * n_tok * TOKEN_ROWS, kv_spec=kv_main)
    r_small = n_small + N_META
    kv_small = pl.BlockSpec((r_small * TOKEN_ROWS, LANES), lambda i, j: (0, 0))
    proj_small = _proj(x_small, n1, w_main, w_a1, w_a2p, ba, *_rope_tables(pos_small),
                       tm=r_small, n_seq=1, kv_rows=r_small * TOKEN_ROWS, kv_spec=kv_small)
    qg, kg, vg, gk, go, qd, kd, vd, kf, vf = proj_main
    qg_s, kg_s, vg_s, gk_s, go_s, qd_s, kd_s, vd_s, kf_s, vf_s = proj_small

    def small_seq(a):
        new = jnp.pad(a[:n_small].reshape(bd, td, -1), ((0, 0), (0, N_META - td), (0, 0)))
        return jnp.concatenate([new, a[n_small:][None]], axis=0)

    s0_small = jnp.concatenate([state_gla[0].astype(F32),
                                jnp.zeros((1,) + state_gla.shape[2:], F32)], axis=0)
    mg_small, st_small = _gla(small_seq(qg_s), small_seq(kg_s), small_seq(vg_s), small_seq(gk_s),
                              small_seq(go_s), s0_small.swapaxes(-1, -2), gnw,
                              chunk=N_META, n_chunks=1, n_seq=3)
    seq = lambda a: a.reshape(b, s, -1)
    s0_main = jnp.broadcast_to(st_small[bd:], (b,) + st_small.shape[1:])
    mg_main, st_main = _gla(seq(qg), seq(kg), seq(vg), seq(gk), seq(go), s0_main, gnw,
                            chunk=GLA_CHUNK, n_chunks=8, n_seq=b, unroll=4)

    md_main = _diff_prompt(seq(qd), seq(kd), seq(vd), kd_s[n_small:], vd_s[n_small:], lams, dnw,
                           lam_init, tq=1024, sub=128)

    n_sub = 2 * DIFF_HEADS
    q4 = jnp.pad(qd_s[:n_small].reshape(bd, td, n_sub, DIFF_HD),
                 ((0, 0), (0, Q_ROWS - td), (0, 0), (0, 0))).transpose(0, 2, 1, 3)
    qbd = (q4[:, :, :, None, :] * jnp.eye(n_sub, dtype=BF16)[None, :, None, :, None])
    qbd = qbd.reshape(bd, n_sub * Q_ROWS, n_sub * DIFF_HD)
    pad_new = lambda a: jnp.pad(a[:n_small].reshape(bd, td, -1), ((0, 0), (0, NEW_PAD - td), (0, 0)))
    halves = DIFF_DV // LANES
    cache_k2 = cache_k[0].reshape(n_phys * PAGE_SIZE * TOKEN_ROWS, LANES)
    cache_v2 = (cache_v[0].reshape(n_phys * PAGE_SIZE, DIFF_HEADS, halves, LANES).transpose(0, 2, 1, 3)
                .reshape(n_phys * PAGE_SIZE * TOKEN_ROWS, LANES))
    decode = (page_table, qbd, cache_k2, cache_v2, pad_new(kd_s), pad_new(vd_s), lams, dnw, lam_init, td)

    x1_main, h2_main = _out_proj(x_main, mg_main.reshape(b * s, -1), md_main.reshape(b * s, -1),
                                 wo, n2, tm=256)
    y_main, od_small = _ffn(h2_main, wg, wu, wd, x1_main, fw, tm=512, tf=256, decode=decode)
    md_small = od_small[:, :td].reshape(n_small, -1).astype(BF16)
    x1_small, h2_small = _out_proj(x_new, mg_small[:bd, :td].reshape(n_small, -1), md_small, wo, n2,
                                   tm=n_small)
    y_small, = _ffn(h2_small, wg, wu, wd, x1_small, fw, tm=n_small, tf=512)

    meta_rows = slice(n_small * TOKEN_ROWS, r_small * TOKEN_ROWS)
    for i in range(b):
        at = (i * n_tok * TOKEN_ROWS, 0)
        kf = lax.dynamic_update_slice(kf, kf_s[meta_rows], at)
        vf = lax.dynamic_update_slice(vf, vf_s[meta_rows], at)
    k_prompt = kf.reshape(1, b, n_tok, DIFF_HEADS, 2, DIFF_HD)
    v_prompt = _value_rows_to_heads(vf, b * n_tok).reshape(1, b, n_tok, DIFF_HEADS, DIFF_DV)
    gla_prompt = st_main.swapaxes(-1, -2)[None].astype(state_gla.dtype)
    k_sample = kf_s[:n_small * TOKEN_ROWS].reshape(1, bd, td, DIFF_HEADS, 2, DIFF_HD)
    v_sample = _value_rows_to_heads(vf_s[:n_small * TOKEN_ROWS], n_small).reshape(
        1, bd, td, DIFF_HEADS, DIFF_DV)
    gla_sample = st_small[:bd].swapaxes(-1, -2)[None].astype(state_gla.dtype)
    return (y_main.reshape(b, s, d), y_small.reshape(bd, td, d), k_prompt, v_prompt, gla_prompt,
            k_sample, v_sample, gla_sample)
```

```python
import functools
import math

import jax
import jax.numpy as jnp
from jax import lax
from jax.experimental import pallas as pl
from jax.experimental.pallas import tpu as pltpu

F32 = jnp.float32
BF16 = jnp.bfloat16

N_META = 16
GLA_HEADS = 4
GLA_DK = 128
GLA_DV = 256
GLA_GATE_RANK = 16
GLA_GATE_NORM = 16.0
GLA_CHUNK = 64
DIFF_HEADS = 4
DIFF_HD = 128
DIFF_DV = 256
ROT_DIM = DIFF_HD // 4
ROPE_THETA = 500000.0
PAGE_SIZE = 128
EPS = 1e-6
NEG_BIG = -1e30
LOG2E = math.log2(math.e)

LANES = 128
TOKEN_ROWS = 2 * DIFF_HEADS
VMEM_LIMIT = 56 * 1024 * 1024


def _dot(a, b):
    return jnp.dot(a, b, preferred_element_type=F32)


def _dot_nt(a, b):
    return lax.dot_general(a, b, (((1,), (1,)), ((), ())), preferred_element_type=F32)


def _dot_tn(a, b):
    return lax.dot_general(a, b, (((0,), (0,)), ((), ())), preferred_element_type=F32)


def _rms(x):
    return lax.rsqrt(jnp.mean(x * x, axis=-1, keepdims=True) + EPS)


def _silu(x):
    return x * jax.nn.sigmoid(x)


def _lambda(lq1, lk1, lq2, lk2, lam_init):
    return (jnp.exp(jnp.sum(lq1[...] * lk1[...], axis=-1, keepdims=True))
            - jnp.exp(jnp.sum(lq2[...] * lk2[...], axis=-1, keepdims=True)) + lam_init)


def _lane_fold(p):
    acc = p[:, :LANES]
    for i in range(1, p.shape[1] // LANES):
        acc = acc + p[:, i * LANES:(i + 1) * LANES]
    return acc


def _regroup_kernel(w_ref, main_ref, gate_ref, *, g0, rank):
    x = w_ref[...]
    main_ref[:, :g0] = x[:, :g0].astype(BF16)
    main_ref[:, g0:] = x[:, g0 + rank:].astype(BF16)
    a = x[:, g0:g0 + LANES]
    lane = lax.broadcasted_iota(jnp.int32, a.shape, 1)
    gate_ref[...] = jnp.where(lane < rank, a, 0.0).astype(BF16)


def _regroup_w_in(w, g0, rank, tr):
    d, n = w.shape
    return pl.pallas_call(
        functools.partial(_regroup_kernel, g0=g0, rank=rank),
        grid=(d // tr,),
        in_specs=[pl.BlockSpec((tr, n), lambda i: (i, 0))],
        out_specs=[pl.BlockSpec((tr, n - rank), lambda i: (i, 0)), pl.BlockSpec((tr, LANES), lambda i: (i, 0))],
        out_shape=[jax.ShapeDtypeStruct((d, n - rank), BF16), jax.ShapeDtypeStruct((d, LANES), BF16)],
        compiler_params=pltpu.CompilerParams(dimension_semantics=("parallel",),
                                             vmem_limit_bytes=VMEM_LIMIT),
        name="regroup_w_in",
    )(w)


def _proj_kernel(x_ref, n1_ref, w_ref, wa_ref, wa2_ref, ba_ref, c_ref, s1_ref, s2_ref,
                 qg_ref, kg_ref, vg_ref, gk_ref, go_ref, qd_ref, kd_ref, vd_ref, kf_ref, vf_ref):
    x = x_ref[...]
    tm = x.shape[0]
    h = (x * _rms(x) * n1_ref[...]).astype(BF16)
    cw = 512

    def mm(c):
        return _dot(h, w_ref[:, c * cw:(c + 1) * cw])

    qg_ref[...] = (mm(0) * (GLA_DK ** -0.5)).astype(BF16)
    kg_ref[...] = mm(1).astype(BF16)
    for c in range(2):
        vg_ref[:, c * cw:(c + 1) * cw] = mm(2 + c).astype(BF16)
        go_ref[:, c * cw:(c + 1) * cw] = mm(4 + c).astype(BF16)

    a_lr = _dot(h, wa_ref[...]).astype(BF16)
    z = _dot(a_lr, wa2_ref[...]) + ba_ref[...]
    gk_ref[...] = (jnp.minimum(z, 0.0) - jnp.log(1.0 + jnp.exp(-jnp.abs(z)))) * (1.0 / GLA_GATE_NORM)

    reps = cw // LANES
    cos = jnp.tile(c_ref[...], (1, reps))
    sin_lo = jnp.tile(s1_ref[...], (1, reps))
    sin_hi = jnp.tile(s2_ref[...], (1, reps))

    def rope(v):
        return (v * cos + pltpu.roll(v, ROT_DIM // 2, 1) * sin_lo
                + pltpu.roll(v, cw - ROT_DIM // 2, 1) * sin_hi)

    sub_per_chunk = cw // DIFF_HD
    heads_per_chunk = cw // DIFF_DV
    for c in range(2):
        cols = slice(c * cw, (c + 1) * cw)
        qd_ref[:, cols] = (rope(mm(6 + c)) * (DIFF_HD ** -0.5 * LOG2E)).astype(BF16)
        kr = rope(mm(8 + c))
        kd_ref[:, cols] = kr.astype(BF16)
        for i in range(sub_per_chunk):
            kf_ref[pl.ds(c * sub_per_chunk + i, tm, stride=TOKEN_ROWS), :] = kr[:, i * LANES:(i + 1) * LANES]
        vv = mm(10 + c)
        vd_ref[:, cols] = vv.astype(BF16)
        for i in range(heads_per_chunk):
            for half in range(DIFF_DV // LANES):
                lo = i * DIFF_DV + half * LANES
                vf_ref[pl.ds(half * DIFF_HEADS + c * heads_per_chunk + i, tm, stride=TOKEN_ROWS), :] = (
                    vv[:, lo:lo + LANES])


def _proj(x, n1, w, wa, wa2, ba, cos, s1, s2, tm, n_seq, kv_rows, kv_spec):
    r, d = x.shape
    ncol = w.shape[1]
    nb = r // n_seq // tm
    row = lambda width: pl.BlockSpec((tm, width), lambda b, i: (b * nb + i, 0))
    tab = pl.BlockSpec((tm, LANES), lambda b, i: (i, 0))
    full = lambda a: pl.BlockSpec(a.shape, lambda b, i: (0, 0))
    widths = (512, 512, 1024, 512, 1024, 1024, 1024, 1024)
    dtypes = (BF16, BF16, BF16, F32, BF16, BF16, BF16, BF16)
    kv_shape = jax.ShapeDtypeStruct((kv_rows, LANES), F32)
    return pl.pallas_call(
        _proj_kernel,
        grid=(n_seq, nb),
        in_specs=[row(d), full(n1),
                  pl.BlockSpec((d, ncol), lambda b, i: (0, 0), pipeline_mode=pl.Buffered(1)),
                  full(wa), full(wa2), full(ba), tab, tab, tab],
        out_specs=[row(wd) for wd in widths] + [kv_spec, kv_spec],
        out_shape=[jax.ShapeDtypeStruct((r, wd), dt) for wd, dt in zip(widths, dtypes)]
        + [kv_shape, kv_shape],
        compiler_params=pltpu.CompilerParams(dimension_semantics=("parallel", "parallel"),
                                             vmem_limit_bytes=VMEM_LIMIT),
        name="proj",
    )(x, n1, w, wa, wa2, ba, cos, s1, s2)


def _gla_kernel(q_ref, k_ref, v_ref, gk_ref, go_ref, s0_ref, nw_ref, o_ref, st_ref, s_scr,
                *, chunk, n_chunks, unroll):
    t = pl.program_id(1)
    n_seq = q_ref.shape[0]

    @pl.when(t == 0)
    def _():
        for si in range(n_seq):
            for h in range(GLA_HEADS):
                s_scr[si, h] = s0_ref[si, h].T

    c = chunk
    row = lax.broadcasted_iota(jnp.int32, (c, c), 0)
    col = lax.broadcasted_iota(jnp.int32, (c, c), 1)
    tril = col <= row
    tri = jnp.where(tril, 1.0, 0.0).astype(BF16)

    def seq_chunk(si, rows):
        g = gk_ref[si, rows, :]
        g_hi = g.astype(BF16)
        g_lo = (g - g_hi.astype(F32)).astype(BF16)
        bcum = _dot(tri, g_hi) + _dot(tri, g_lo)
        b_mid = bcum[c // 2:c // 2 + 1, :]
        b_last = bcum[c - 1:c, :]
        q = q_ref[si, rows, :].astype(F32)
        k = k_ref[si, rows, :].astype(F32)
        qe = (q * jnp.exp(bcum - b_mid)).astype(BF16)
        ke = (k * jnp.exp(b_mid - bcum)).astype(BF16)
        qs = (q * jnp.exp(bcum)).astype(BF16)
        kl = (k * jnp.exp(b_last - bcum)).astype(BF16)
        decay = jnp.exp(b_last)
        for h in range(GLA_HEADS):
            hk = slice(h * GLA_DK, (h + 1) * GLA_DK)
            hv = slice(h * GLA_DV, (h + 1) * GLA_DV)
            vh = v_ref[si, rows, hv]
            a = jnp.where(tril, _dot_nt(qe[:, hk], ke[:, hk]), 0.0)
            st = s_scr[si, h]
            o = _dot(a.astype(BF16), vh) + _dot_nt(qs[:, hk], st.astype(BF16))
            s_scr[si, h] = decay[:, hk] * st + _dot_tn(vh, kl[:, hk])
            gate = go_ref[si, rows, hv].astype(F32)
            o_ref[si, rows, hv] = (o * _rms(o) * nw_ref[...] * _silu(gate)).astype(BF16)

    def chunk_body(ci, carry):
        rows = pl.ds(pl.multiple_of(ci * c, c), c)
        for si in range(n_seq):
            seq_chunk(si, rows)
        return carry

    lax.fori_loop(0, n_chunks, chunk_body, 0, unroll=unroll)

    @pl.when(t == pl.num_programs(1) - 1)
    def _():
        for si in range(n_seq):
            for h in range(GLA_HEADS):
                st_ref[si, h] = s_scr[si, h].T


def _gla(q, k, v, gk, go, s0, nw, chunk, n_chunks, n_seq, unroll=1):
    b, t, _ = q.shape
    tc = chunk * n_chunks
    seq = lambda width: pl.BlockSpec((n_seq, tc, width), lambda i, j: (i, j, 0))
    state = pl.BlockSpec((n_seq, GLA_HEADS, GLA_DK, GLA_DV), lambda i, j: (i, 0, 0, 0))
    return pl.pallas_call(
        functools.partial(_gla_kernel, chunk=chunk, n_chunks=n_chunks, unroll=unroll),
        grid=(b // n_seq, t // tc),
        in_specs=[seq(512), seq(512), seq(1024), seq(512), seq(1024), state,
                  pl.BlockSpec(nw.shape, lambda i, j: (0, 0))],
        out_specs=[seq(1024), state],
        out_shape=[jax.ShapeDtypeStruct((b, t, 1024), BF16),
                   jax.ShapeDtypeStruct((b, GLA_HEADS, GLA_DK, GLA_DV), F32)],
        scratch_shapes=[pltpu.VMEM((n_seq, GLA_HEADS, GLA_DV, GLA_DK), F32)],
        compiler_params=pltpu.CompilerParams(dimension_semantics=("parallel", "arbitrary"),
                                             vmem_limit_bytes=VMEM_LIMIT),
        name="gla",
    )(q, k, v, gk, go, s0, nw)


def _diff_prompt_kernel(q_ref, k_ref, v_ref, km_ref, vm_ref, lq1, lk1, lq2, lk2, nw_ref, o_ref,
                        m_scr, l_scr, acc_scr, *, tq, sub, lam_init):
    qi = pl.program_id(2)
    n_sub = tq // sub

    for c in range(2):
        cs = slice(c * DIFF_HD, (c + 1) * DIFF_HD)
        s = _dot_nt(q_ref[:, cs], km_ref[:, cs])
        m_new = jnp.max(s, axis=-1, keepdims=True)
        p = jnp.exp2(s - m_new)
        m_scr[c] = jnp.broadcast_to(m_new, (tq, LANES))
        l_scr[c] = jnp.broadcast_to(jnp.sum(p, axis=-1, keepdims=True) * (1.0 / LANES), (tq, LANES))
        acc_scr[c] = _dot(p.astype(BF16), vm_ref[...])

    def update(c, rs, s, vb):
        m_prev = m_scr[c, rs, :]
        m_new = jnp.maximum(m_prev, jnp.max(s, axis=-1, keepdims=True))
        alpha = jnp.exp2(m_prev - m_new)
        p = jnp.exp2(s - jnp.tile(m_new, (1, s.shape[1] // LANES)))
        l_scr[c, rs, :] = alpha * l_scr[c, rs, :] + _lane_fold(p)
        acc_scr[c, rs, :] = (jnp.tile(alpha, (1, DIFF_DV // LANES)) * acc_scr[c, rs, :]
                             + _dot(p.astype(BF16), vb))
        m_scr[c, rs, :] = m_new

    def body(j, carry):
        rows = pl.ds(pl.multiple_of(j * tq, tq), tq)
        vb = v_ref[rows, :]
        for c in range(2):
            cs = slice(c * DIFF_HD, (c + 1) * DIFF_HD)
            kc = k_ref[rows, cs]
            for r in range(n_sub):
                rs = slice(r * sub, (r + 1) * sub)
                update(c, rs, _dot_nt(q_ref[rs, cs], kc), vb)
        return carry

    lax.fori_loop(0, qi, body, 0)

    base = pl.multiple_of(qi * tq, tq)
    for c in range(2):
        cs = slice(c * DIFF_HD, (c + 1) * DIFF_HD)
        for r in range(n_sub):
            rs = slice(r * sub, (r + 1) * sub)
            nk = (r + 1) * sub
            rows = pl.ds(base, nk)
            s = _dot_nt(q_ref[rs, cs], k_ref[rows, cs])
            q_pos = lax.broadcasted_iota(jnp.int32, (sub, nk), 0) + r * sub
            k_pos = lax.broadcasted_iota(jnp.int32, (sub, nk), 1)
            update(c, rs, jnp.where(k_pos <= q_pos, s, NEG_BIG), v_ref[rows, :])

    lam = _lambda(lq1, lk1, lq2, lk2, lam_init)
    a1 = acc_scr[0] / jnp.sum(l_scr[0], axis=-1, keepdims=True)
    a2 = acc_scr[1] / jnp.sum(l_scr[1], axis=-1, keepdims=True)
    o = a1 - lam * a2
    o_ref[...] = (o * _rms(o) * nw_ref[...] * (1.0 - lam_init)).astype(BF16)


def _diff_prompt(q, k, v, km, vm, lams, nw, lam_init, tq, sub):
    b, s, _ = q.shape
    hw = 2 * DIFF_HD
    small = lambda a: pl.BlockSpec(a.shape, lambda i, h, j: (0, 0))
    return pl.pallas_call(
        functools.partial(_diff_prompt_kernel, tq=tq, sub=sub, lam_init=lam_init),
        grid=(b, DIFF_HEADS, s // tq),
        in_specs=[pl.BlockSpec((None, tq, hw), lambda i, h, j: (i, j, h)),
                  pl.BlockSpec((None, s, hw), lambda i, h, j: (i, 0, h)),
                  pl.BlockSpec((None, s, DIFF_DV), lambda i, h, j: (i, 0, h)),
                  pl.BlockSpec((N_META, hw), lambda i, h, j: (0, h)),
                  pl.BlockSpec((N_META, DIFF_DV), lambda i, h, j: (0, h)),
                  small(lams[0]), small(lams[1]), small(lams[2]), small(lams[3]), small(nw)],
        out_specs=pl.BlockSpec((None, tq, DIFF_DV), lambda i, h, j: (i, j, h)),
        out_shape=jax.ShapeDtypeStruct((b, s, DIFF_HEADS * DIFF_DV), BF16),
        scratch_shapes=[pltpu.VMEM((2, tq, LANES), F32), pltpu.VMEM((2, tq, LANES), F32),
                        pltpu.VMEM((2, tq, DIFF_DV), F32)],
        compiler_params=pltpu.CompilerParams(
            dimension_semantics=("parallel", "parallel", "arbitrary"),
            vmem_limit_bytes=VMEM_LIMIT),
        name="diff_prompt",
    )(q, k, v, km, vm, *lams, nw)


Q_ROWS = 8
NEW_PAD = 16


def _decode_init(m_scr, l_scr, acc_scr):
    m_scr[...] = jnp.full(m_scr.shape, NEG_BIG, F32)
    l_scr[...] = jnp.zeros(l_scr.shape, F32)
    acc_scr[...] = jnp.zeros(acc_scr.shape, F32)


def _decode_update(m_scr, l_scr, acc_scr, s, vb):
    m_prev = m_scr[...][:, :1]
    m_new = jnp.maximum(m_prev, jnp.max(s, axis=-1, keepdims=True))
    alpha = jnp.exp2(m_prev - m_new)
    p = jnp.exp2(s - m_new)
    l_scr[...] = alpha * l_scr[...] + jnp.sum(p, axis=-1, keepdims=True)
    acc_scr[...] = alpha * acc_scr[...] + _dot(p.astype(BF16), vb)
    m_scr[...] = jnp.broadcast_to(m_new, m_scr.shape)


def _decode_pages(q_ref, k_refs, v_refs, m_scr, l_scr, acc_scr):
    def tile_row(ref, r):
        return ref[pl.ds(r, PAGE_SIZE, stride=TOKEN_ROWS), :].astype(BF16)

    def k_page(ref):
        return jnp.concatenate([tile_row(ref, hc) for hc in range(TOKEN_ROWS)], axis=1)

    def v_page(ref):
        return jnp.concatenate([tile_row(ref, half * DIFF_HEADS + h) for h in range(DIFF_HEADS)
                                for half in range(DIFF_DV // LANES)], axis=1)

    kg = jnp.concatenate([k_page(r) for r in k_refs], axis=0)
    vg = jnp.concatenate([v_page(r) for r in v_refs], axis=0)
    s = _dot_nt(q_ref[...], kg)
    _decode_update(m_scr, l_scr, acc_scr, s, vg)


def _decode_finish(q_ref, kn_ref, vn_ref, lam_refs, nw_ref, o_ref, m_scr, l_scr, acc_scr,
                   *, n_new, lam_init):
    s = _dot_nt(q_ref[...], kn_ref[...])
    t = lax.broadcasted_iota(jnp.int32, s.shape, 0) % Q_ROWS
    j = lax.broadcasted_iota(jnp.int32, s.shape, 1)
    s = jnp.where(j <= jnp.minimum(t, n_new - 1), s, NEG_BIG)
    _decode_update(m_scr, l_scr, acc_scr, s, vn_ref[...])
    lam = _lambda(*lam_refs, lam_init)
    out = acc_scr[...] / l_scr[...][:, :1]
    for h in range(DIFF_HEADS):
        r0 = h * 2 * Q_ROWS
        cols = slice(h * DIFF_DV, (h + 1) * DIFF_DV)
        o = out[r0:r0 + Q_ROWS, cols] - lam * out[r0 + Q_ROWS:r0 + 2 * Q_ROWS, cols]
        o_ref[:, cols] = o * _rms(o) * nw_ref[...] * (1.0 - lam_init)


def _diff_sample_kernel(pt_ref, q_ref, *refs, group, n_new, lam_init):
    del pt_ref
    k_refs = refs[:group]
    v_refs = refs[group:2 * group]
    kn_ref, vn_ref, lq1, lk1, lq2, lk2, nw_ref, o_ref, m_scr, l_scr, acc_scr = refs[2 * group:]
    g = pl.program_id(1)
    scr = (m_scr, l_scr, acc_scr)
    pl.when(g == 0)(functools.partial(_decode_init, *scr))
    _decode_pages(q_ref, k_refs, v_refs, *scr)

    @pl.when(g == pl.num_programs(1) - 1)
    def _():
        _decode_finish(q_ref, kn_ref, vn_ref, (lq1, lk1, lq2, lk2), nw_ref, o_ref, *scr,
                       n_new=n_new, lam_init=lam_init)


def _diff_sample(page_table, qbd, cache_k, cache_v, k_new, v_new, lams, nw, lam_init, n_new, group):
    bd, n_pages = page_table.shape
    nrow = qbd.shape[1]
    vw = DIFF_HEADS * DIFF_DV
    assert n_pages % group == 0

    def page_map(b, g, pt, i):
        return (pt[b, g * group + i], 0)

    per_b = lambda a: pl.BlockSpec((None,) + a.shape[1:], lambda b, g, pt: (b, 0, 0))
    small = lambda a: pl.BlockSpec(a.shape, lambda b, g, pt: (0, 0))
    page_specs = [pl.BlockSpec((PAGE_SIZE * TOKEN_ROWS, LANES), functools.partial(page_map, i=i))
                  for i in range(group)]
    in_specs = ([per_b(qbd)] + page_specs + page_specs
                + [per_b(k_new), per_b(v_new)] + [small(a) for a in lams] + [small(nw)])
    return pl.pallas_call(
        functools.partial(_diff_sample_kernel, group=group, n_new=n_new, lam_init=lam_init),
        grid_spec=pltpu.PrefetchScalarGridSpec(
            num_scalar_prefetch=1,
            grid=(bd, n_pages // group),
            in_specs=in_specs,
            out_specs=pl.BlockSpec((None, Q_ROWS, vw), lambda b, g, pt: (b, 0, 0)),
            scratch_shapes=[pltpu.VMEM((nrow, LANES), F32), pltpu.VMEM((nrow, LANES), F32),
                            pltpu.VMEM((nrow, vw), F32)]),
        out_shape=jax.ShapeDtypeStruct((bd, Q_ROWS, vw), F32),
        compiler_params=pltpu.CompilerParams(dimension_semantics=("parallel", "arbitrary"),
                                             vmem_limit_bytes=VMEM_LIMIT),
        name="diff_sample",
    )(page_table, qbd, *([cache_k] * group), *([cache_v] * group), k_new, v_new, *lams, nw)


def _out_proj_kernel(x_ref, mg_ref, md_ref, wo_ref, n2_ref, x1_ref, h2_ref):
    half = mg_ref.shape[1]
    y = x_ref[...] + _dot(mg_ref[...], wo_ref[:half, :]) + _dot(md_ref[...], wo_ref[half:, :])
    x1_ref[...] = y
    h2_ref[...] = (y * _rms(y) * n2_ref[...]).astype(BF16)


def _out_proj(x, mg, md, wo, n2, tm):
    r, d = x.shape
    row = lambda a: pl.BlockSpec((tm, a.shape[1]), lambda i: (i, 0))
    full = lambda a: pl.BlockSpec(a.shape, lambda i: (0, 0))
    return pl.pallas_call(
        _out_proj_kernel,
        grid=(r // tm,),
        in_specs=[row(x), row(mg), row(md), full(wo), full(n2)],
        out_specs=[row(x), row(x)],
        out_shape=[jax.ShapeDtypeStruct((r, d), F32), jax.ShapeDtypeStruct((r, d), BF16)],
        compiler_params=pltpu.CompilerParams(dimension_semantics=("parallel",),
                                             vmem_limit_bytes=VMEM_LIMIT),
        name="out_proj",
    )(x, mg, md, wo, n2)


def _ffn_kernel(h_ref, wg_ref, wu_ref, wd_ref, x_ref, fw_ref, y_ref, acc_scr):
    j = pl.program_id(1)

    @pl.when(j == 0)
    def _():
        acc_scr[...] = jnp.zeros(acc_scr.shape, F32)

    h = h_ref[...]
    act = (_silu(_dot(h, wg_ref[...])) * _dot(h, wu_ref[...])).astype(BF16)
    acc_scr[...] += _dot(act, wd_ref[...])

    @pl.when(j == pl.num_programs(1) - 1)
    def _():
        y = x_ref[...] + acc_scr[...]
        y_ref[...] = y * _rms(y) * fw_ref[...]


def _ffn(h2, wg, wu, wd, x1, fw, tm, tf):
    r, d = h2.shape
    f = wg.shape[1]
    return pl.pallas_call(
        _ffn_kernel,
        grid=(r // tm, f // tf),
        in_specs=[pl.BlockSpec((tm, d), lambda i, j: (i, 0)),
                  pl.BlockSpec((d, tf), lambda i, j: (0, j)),
                  pl.BlockSpec((d, tf), lambda i, j: (0, j)),
                  pl.BlockSpec((tf, d), lambda i, j: (j, 0)),
                  pl.BlockSpec((tm, d), lambda i, j: (i, 0)),
                  pl.BlockSpec(fw.shape, lambda i, j: (0, 0))],
        out_specs=pl.BlockSpec((tm, d), lambda i, j: (i, 0)),
        out_shape=jax.ShapeDtypeStruct((r, d), F32),
        scratch_shapes=[pltpu.VMEM((tm, d), F32)],
        compiler_params=pltpu.CompilerParams(dimension_semantics=("parallel", "arbitrary"),
                                             vmem_limit_bytes=VMEM_LIMIT),
        name="ffn",
    )(h2, wg, wu, wd, x1, fw)


def _rope_tables(pos):
    half = ROT_DIM // 2
    inv = ROPE_THETA ** (-jnp.arange(half, dtype=F32) * 2.0 / ROT_DIM)
    ang = pos.astype(F32)[:, None] * inv[None, :]
    cos, sin = jnp.cos(ang), jnp.sin(ang)
    n = pos.shape[0]
    zeros = lambda w: jnp.zeros((n, w), F32)
    c = jnp.concatenate([cos, cos, jnp.ones((n, LANES - ROT_DIM), F32)], axis=1)
    s_lo = jnp.concatenate([zeros(half), sin, zeros(LANES - ROT_DIM)], axis=1)
    s_hi = jnp.concatenate([-sin, zeros(LANES - half)], axis=1)
    return c, s_lo, s_hi


def _value_rows_to_heads(v2d, n_tok):
    halves = DIFF_DV // LANES
    v = v2d.reshape(n_tok, halves, DIFF_HEADS, LANES).transpose(0, 2, 1, 3)
    return v.reshape(n_tok, DIFF_HEADS, DIFF_DV)


def kernel(x_prompt, x_sample, cache_k, cache_v, state_gla, page_table, meta_tokens, norm1_w, w_in,
           w_a2, b_a, gla_norm_w, lambda_q1, lambda_k1, lambda_q2, lambda_k2, diff_norm_w, w_o,
           norm2_w, w_gate, w_up, w_down, final_norm_w):
    b, s, d = x_prompt.shape
    bd, td, _ = x_sample.shape
    assert norm1_w.shape[0] == 1, "single-layer step only"
    assert td <= Q_ROWS and td <= N_META
    n_phys = cache_k.shape[1]
    past_len = page_table.shape[1] * PAGE_SIZE
    lam_init = 0.8 - 0.6 * math.exp(-0.3 * 0)
    n_small = bd * td
    n_tok = N_META + s

    g0 = 2 * GLA_HEADS * GLA_DK + GLA_HEADS * GLA_DV
    w_main, w_a1 = _regroup_w_in(w_in[0], g0, GLA_GATE_RANK, tr=256)
    w_a2p = jnp.pad(w_a2[0], ((0, LANES - GLA_GATE_RANK), (0, 0))).astype(BF16)
    row2 = lambda a: a.reshape(1, -1).astype(F32)
    n1, ba, gnw, dnw, n2, fw = (row2(norm1_w[0]), row2(b_a[0]), row2(gla_norm_w[0]),
                                row2(diff_norm_w[0]), row2(norm2_w[0]), row2(final_norm_w))
    lams = tuple(row2(a[0]) for a in (lambda_q1, lambda_k1, lambda_q2, lambda_k2))
    wo, wg, wu, wd = (w_o[0].astype(BF16), w_gate[0].astype(BF16), w_up[0].astype(BF16),
                      w_down[0].astype(BF16))

    x_main = x_prompt.reshape(b * s, d)
    x_new = x_sample.reshape(n_small, d)
    x_small = jnp.concatenate([x_new, meta_tokens.astype(F32)], axis=0)
    pos_small = jnp.concatenate([past_len + jnp.arange(n_small) % td, jnp.arange(N_META)])

    tm = 256
    kv_main = pl.BlockSpec((pl.Element(tm * TOKEN_ROWS), pl.Element(LANES)),
                           lambda i, j: ((i * n_tok + N_META + j * tm) * TOKEN_ROWS, 0))
    proj_main = _proj(x_main, n1, w_main, w_a1, w_a2p, ba, *_rope_tables(N_META + jnp.arange(s)),
                      tm=tm, n_seq=b, kv_rows=b * n_tok * TOKEN_ROWS, kv_spec=kv_main)
    r_small = n_small + N_META
    kv_small = pl.BlockSpec((r_small * TOKEN_ROWS, LANES), lambda i, j: (0, 0))
    proj_small = _proj(x_small, n1, w_main, w_a1, w_a2p, ba, *_rope_tables(pos_small),
                       tm=r_small, n_seq=1, kv_rows=r_small * TOKEN_ROWS, kv_spec=kv_small)
    qg, kg, vg, gk, go, qd, kd, vd, kf, vf = proj_main
    qg_s, kg_s, vg_s, gk_s, go_s, qd_s, kd_s, vd_s, kf_s, vf_s = proj_small

    def small_seq(a):
        new = jnp.pad(a[:n_small].reshape(bd, td, -1), ((0, 0), (0, N_META - td), (0, 0)))
        return jnp.concatenate([new, a[n_small:][None]], axis=0)

    s0_small = jnp.concatenate([state_gla[0].astype(F32),
                                jnp.zeros((1,) + state_gla.shape[2:], F32)], axis=0)
    mg_small, st_small = _gla(small_seq(qg_s), small_seq(kg_s), small_seq(vg_s), small_seq(gk_s),
                              small_seq(go_s), s0_small, gnw,
                              chunk=N_META, n_chunks=1, n_seq=3)
    seq = lambda a: a.reshape(b, s, -1)
    s0_main = jnp.broadcast_to(st_small[bd:], (b,) + st_small.shape[1:])
    mg_main, st_main = _gla(seq(qg), seq(kg), seq(vg), seq(gk), seq(go), s0_main, gnw,
                            chunk=GLA_CHUNK, n_chunks=8, n_seq=b, unroll=4)

    md_main = _diff_prompt(seq(qd), seq(kd), seq(vd), kd_s[n_small:], vd_s[n_small:], lams, dnw,
                           lam_init, tq=2048, sub=128)

    n_sub = 2 * DIFF_HEADS
    q4 = jnp.pad(qd_s[:n_small].reshape(bd, td, n_sub, DIFF_HD),
                 ((0, 0), (0, Q_ROWS - td), (0, 0), (0, 0))).transpose(0, 2, 1, 3)
    qbd = (q4[:, :, :, None, :] * jnp.eye(n_sub, dtype=BF16)[None, :, None, :, None])
    qbd = qbd.reshape(bd, n_sub * Q_ROWS, n_sub * DIFF_HD)
    pad_new = lambda a: jnp.pad(a[:n_small].reshape(bd, td, -1), ((0, 0), (0, NEW_PAD - td), (0, 0)))
    halves = DIFF_DV // LANES
    cache_k2 = cache_k[0].reshape(n_phys * PAGE_SIZE * TOKEN_ROWS, LANES)
    cache_v2 = (cache_v[0].reshape(n_phys * PAGE_SIZE, DIFF_HEADS, halves, LANES).transpose(0, 2, 1, 3)
                .reshape(n_phys * PAGE_SIZE * TOKEN_ROWS, LANES))
    od_small = _diff_sample(page_table, qbd, cache_k2, cache_v2, pad_new(kd_s), pad_new(vd_s), lams,
                            dnw, lam_init, n_new=td, group=16)
    md_small = od_small[:, :td].reshape(n_small, -1).astype(BF16)

    x1_main, h2_main = _out_proj(x_main, mg_main.reshape(b * s, -1), md_main.reshape(b * s, -1),
                                 wo, n2, tm=256)
    y_main = _ffn(h2_main, wg, wu, wd, x1_main, fw, tm=512, tf=512)
    x1_small, h2_small = _out_proj(x_new, mg_small[:bd, :td].reshape(n_small, -1), md_small, wo, n2,
                                   tm=n_small)
    y_small = _ffn(h2_small, wg, wu, wd, x1_small, fw, tm=n_small, tf=512)

    meta_rows = slice(n_small * TOKEN_ROWS, r_small * TOKEN_ROWS)
    for i in range(b):
        at = (i * n_tok * TOKEN_ROWS, 0)
        kf = lax.dynamic_update_slice(kf, kf_s[meta_rows], at)
        vf = lax.dynamic_update_slice(vf, vf_s[meta_rows], at)
    k_prompt = kf.reshape(1, b, n_tok, DIFF_HEADS, 2, DIFF_HD)
    v_prompt = _value_rows_to_heads(vf, b * n_tok).reshape(1, b, n_tok, DIFF_HEADS, DIFF_DV)
    gla_prompt = st_main[None].astype(state_gla.dtype)
    k_sample = kf_s[:n_small * TOKEN_ROWS].reshape(1, bd, td, DIFF_HEADS, 2, DIFF_HD)
    v_sample = _value_rows_to_heads(vf_s[:n_small * TOKEN_ROWS], n_small).reshape(
        1, bd, td, DIFF_HEADS, DIFF_DV)
    gla_sample = st_small[:bd][None].astype(state_gla.dtype)
    return (y_main.reshape(b, s, d), y_small.reshape(bd, td, d), k_prompt, v_prompt, gla_prompt,
            k_sample, v_sample, gla_sample)
```

```python
import functools
import math

import jax
import jax.numpy as jnp
from jax import lax
from jax.experimental import pallas as pl
from jax.experimental.pallas import tpu as pltpu

F32 = jnp.float32
BF16 = jnp.bfloat16

N_META = 16
GLA_HEADS = 4
GLA_DK = 128
GLA_DV = 256
GLA_GATE_RANK = 16
GLA_GATE_NORM = 16.0
GLA_CHUNK = 64
DIFF_HEADS = 4
DIFF_HD = 128
DIFF_DV = 256
ROT_DIM = DIFF_HD // 4
ROPE_THETA = 500000.0
PAGE_SIZE = 128
EPS = 1e-6
NEG_BIG = -1e30
LOG2E = math.log2(math.e)

LANES = 128
TOKEN_ROWS = 2 * DIFF_HEADS
VMEM_LIMIT = 56 * 1024 * 1024


def _dot(a, b):
    return jnp.dot(a, b, preferred_element_type=F32)


def _dot_nt(a, b):
    return lax.dot_general(a, b, (((1,), (1,)), ((), ())), preferred_element_type=F32)


def _dot_tn(a, b):
    return lax.dot_general(a, b, (((0,), (0,)), ((), ())), preferred_element_type=F32)


def _rms(x):
    return lax.rsqrt(jnp.mean(x * x, axis=-1, keepdims=True) + EPS)


def _silu(x):
    return x * jax.nn.sigmoid(x)


def _lambda(lq1, lk1, lq2, lk2, lam_init):
    return (jnp.exp(jnp.sum(lq1[...] * lk1[...], axis=-1, keepdims=True))
            - jnp.exp(jnp.sum(lq2[...] * lk2[...], axis=-1, keepdims=True)) + lam_init)


def _lane_fold(p):
    acc = p[:, :LANES]
    for i in range(1, p.shape[1] // LANES):
        acc = acc + p[:, i * LANES:(i + 1) * LANES]
    return acc


def _regroup_kernel(w_ref, main_ref, gate_ref, *, g0, rank):
    x = w_ref[...]
    main_ref[:, :g0] = x[:, :g0].astype(BF16)
    main_ref[:, g0:] = x[:, g0 + rank:].astype(BF16)
    a = x[:, g0:g0 + LANES]
    lane = lax.broadcasted_iota(jnp.int32, a.shape, 1)
    gate_ref[...] = jnp.where(lane < rank, a, 0.0).astype(BF16)


def _regroup_w_in(w, g0, rank, tr):
    d, n = w.shape
    return pl.pallas_call(
        functools.partial(_regroup_kernel, g0=g0, rank=rank),
        grid=(d // tr,),
        in_specs=[pl.BlockSpec((tr, n), lambda i: (i, 0))],
        out_specs=[pl.BlockSpec((tr, n - rank), lambda i: (i, 0)), pl.BlockSpec((tr, LANES), lambda i: (i, 0))],
        out_shape=[jax.ShapeDtypeStruct((d, n - rank), BF16), jax.ShapeDtypeStruct((d, LANES), BF16)],
        compiler_params=pltpu.CompilerParams(dimension_semantics=("parallel",),
                                             vmem_limit_bytes=VMEM_LIMIT),
        name="regroup_w_in",
    )(w)


def _proj_kernel(x_ref, n1_ref, w_ref, wa_ref, wa2_ref, ba_ref, c_ref, s1_ref, s2_ref,
                 qg_ref, kg_ref, vg_ref, gk_ref, go_ref, qd_ref, kd_ref, vd_ref, kf_ref, vf_ref):
    x = x_ref[...]
    tm = x.shape[0]
    h = (x * _rms(x) * n1_ref[...]).astype(BF16)
    cw = 512

    def mm(c):
        return _dot(h, w_ref[:, c * cw:(c + 1) * cw])

    qg_ref[...] = (mm(0) * (GLA_DK ** -0.5)).astype(BF16)
    kg_ref[...] = mm(1).astype(BF16)
    for c in range(2):
        vg_ref[:, c * cw:(c + 1) * cw] = mm(2 + c).astype(BF16)
        go_ref[:, c * cw:(c + 1) * cw] = mm(4 + c).astype(BF16)

    a_lr = _dot(h, wa_ref[...]).astype(BF16)
    z = _dot(a_lr, wa2_ref[...]) + ba_ref[...]
    gk_ref[...] = (jnp.minimum(z, 0.0) - jnp.log(1.0 + jnp.exp(-jnp.abs(z)))) * (1.0 / GLA_GATE_NORM)

    reps = cw // LANES
    cos = jnp.tile(c_ref[...], (1, reps))
    sin_lo = jnp.tile(s1_ref[...], (1, reps))
    sin_hi = jnp.tile(s2_ref[...], (1, reps))

    def rope(v):
        return (v * cos + pltpu.roll(v, ROT_DIM // 2, 1) * sin_lo
                + pltpu.roll(v, cw - ROT_DIM // 2, 1) * sin_hi)

    sub_per_chunk = cw // DIFF_HD
    heads_per_chunk = cw // DIFF_DV
    for c in range(2):
        cols = slice(c * cw, (c + 1) * cw)
        qd_ref[:, cols] = (rope(mm(6 + c)) * (DIFF_HD ** -0.5 * LOG2E)).astype(BF16)
        kr = rope(mm(8 + c))
        kd_ref[:, cols] = kr.astype(BF16)
        for i in range(sub_per_chunk):
            kf_ref[pl.ds(c * sub_per_chunk + i, tm, stride=TOKEN_ROWS), :] = kr[:, i * LANES:(i + 1) * LANES]
        vv = mm(10 + c)
        vd_ref[:, cols] = vv.astype(BF16)
        for i in range(heads_per_chunk):
            for half in range(DIFF_DV // LANES):
                lo = i * DIFF_DV + half * LANES
                vf_ref[pl.ds(half * DIFF_HEADS + c * heads_per_chunk + i, tm, stride=TOKEN_ROWS), :] = (
                    vv[:, lo:lo + LANES])


def _proj(x, n1, w, wa, wa2, ba, cos, s1, s2, tm, n_seq, kv_rows, kv_spec):
    r, d = x.shape
    ncol = w.shape[1]
    nb = r // n_seq // tm
    row = lambda width: pl.BlockSpec((tm, width), lambda b, i: (b * nb + i, 0))
    tab = pl.BlockSpec((tm, LANES), lambda b, i: (i, 0))
    full = lambda a: pl.BlockSpec(a.shape, lambda b, i: (0, 0))
    widths = (512, 512, 1024, 512, 1024, 1024, 1024, 1024)
    dtypes = (BF16, BF16, BF16, F32, BF16, BF16, BF16, BF16)
    kv_shape = jax.ShapeDtypeStruct((kv_rows, LANES), F32)
    return pl.pallas_call(
        _proj_kernel,
        grid=(n_seq, nb),
        in_specs=[row(d), full(n1),
                  pl.BlockSpec((d, ncol), lambda b, i: (0, 0), pipeline_mode=pl.Buffered(1)),
                  full(wa), full(wa2), full(ba), tab, tab, tab],
        out_specs=[row(wd) for wd in widths] + [kv_spec, kv_spec],
        out_shape=[jax.ShapeDtypeStruct((r, wd), dt) for wd, dt in zip(widths, dtypes)]
        + [kv_shape, kv_shape],
        compiler_params=pltpu.CompilerParams(dimension_semantics=("parallel", "parallel"),
                                             vmem_limit_bytes=VMEM_LIMIT),
        name="proj",
    )(x, n1, w, wa, wa2, ba, cos, s1, s2)


def _gla_kernel(q_ref, k_ref, v_ref, gk_ref, go_ref, s0_ref, nw_ref, o_ref, st_ref, s_scr,
                *, chunk, n_chunks, unroll):
    t = pl.program_id(1)
    n_seq = q_ref.shape[0]

    @pl.when(t == 0)
    def _():
        for si in range(n_seq):
            for h in range(GLA_HEADS):
                s_scr[si, h] = s0_ref[si, h].T

    c = chunk
    row = lax.broadcasted_iota(jnp.int32, (c, c), 0)
    col = lax.broadcasted_iota(jnp.int32, (c, c), 1)
    tril = col <= row
    tri = jnp.where(tril, 1.0, 0.0).astype(BF16)

    def seq_chunk(si, rows):
        g = gk_ref[si, rows, :]
        g_hi = g.astype(BF16)
        g_lo = (g - g_hi.astype(F32)).astype(BF16)
        bcum = _dot(tri, g_hi) + _dot(tri, g_lo)
        b_mid = bcum[c // 2:c // 2 + 1, :]
        b_last = bcum[c - 1:c, :]
        q = q_ref[si, rows, :].astype(F32)
        k = k_ref[si, rows, :].astype(F32)
        qe = (q * jnp.exp(bcum - b_mid)).astype(BF16)
        ke = (k * jnp.exp(b_mid - bcum)).astype(BF16)
        qs = (q * jnp.exp(bcum)).astype(BF16)
        kl = (k * jnp.exp(b_last - bcum)).astype(BF16)
        decay = jnp.exp(b_last)
        for h in range(GLA_HEADS):
            hk = slice(h * GLA_DK, (h + 1) * GLA_DK)
            hv = slice(h * GLA_DV, (h + 1) * GLA_DV)
            vh = v_ref[si, rows, hv]
            a = jnp.where(tril, _dot_nt(qe[:, hk], ke[:, hk]), 0.0)
            st = s_scr[si, h]
            o = _dot(a.astype(BF16), vh) + _dot_nt(qs[:, hk], st.astype(BF16))
            s_scr[si, h] = decay[:, hk] * st + _dot_tn(vh, kl[:, hk])
            gate = go_ref[si, rows, hv].astype(F32)
            o_ref[si, rows, hv] = (o * _rms(o) * nw_ref[...] * _silu(gate)).astype(BF16)

    def chunk_body(ci, carry):
        rows = pl.ds(pl.multiple_of(ci * c, c), c)
        for si in range(n_seq):
            seq_chunk(si, rows)
        return carry

    lax.fori_loop(0, n_chunks, chunk_body, 0, unroll=unroll)

    @pl.when(t == pl.num_programs(1) - 1)
    def _():
        for si in range(n_seq):
            for h in range(GLA_HEADS):
                st_ref[si, h] = s_scr[si, h].T


def _gla(q, k, v, gk, go, s0, nw, chunk, n_chunks, n_seq, unroll=1):
    b, t, _ = q.shape
    tc = chunk * n_chunks
    seq = lambda width: pl.BlockSpec((n_seq, tc, width), lambda i, j: (i, j, 0))
    state = pl.BlockSpec((n_seq, GLA_HEADS, GLA_DK, GLA_DV), lambda i, j: (i, 0, 0, 0))
    return pl.pallas_call(
        functools.partial(_gla_kernel, chunk=chunk, n_chunks=n_chunks, unroll=unroll),
        grid=(b // n_seq, t // tc),
        in_specs=[seq(512), seq(512), seq(1024), seq(512), seq(1024), state,
                  pl.BlockSpec(nw.shape, lambda i, j: (0, 0))],
        out_specs=[seq(1024), state],
        out_shape=[jax.ShapeDtypeStruct((b, t, 1024), BF16),
                   jax.ShapeDtypeStruct((b, GLA_HEADS, GLA_DK, GLA_DV), F32)],
        scratch_shapes=[pltpu.VMEM((n_seq, GLA_HEADS, GLA_DV, GLA_DK), F32)],
        compiler_params=pltpu.CompilerParams(dimension_semantics=("parallel", "arbitrary"),
                                             vmem_limit_bytes=VMEM_LIMIT),
        name="gla",
    )(q, k, v, gk, go, s0, nw)


def _diff_prompt_kernel(q_ref, k_ref, v_ref, km_ref, vm_ref, lq1, lk1, lq2, lk2, nw_ref, o_ref,
                        m_scr, l_scr, acc_scr, *, tq, sub, lam_init):
    qi = pl.program_id(2)
    n_sub = tq // sub

    for c in range(2):
        cs = slice(c * DIFF_HD, (c + 1) * DIFF_HD)
        s = _dot_nt(q_ref[:, cs], km_ref[:, cs])
        m_new = jnp.max(s, axis=-1, keepdims=True)
        p = jnp.exp2(s - m_new)
        m_scr[c] = jnp.broadcast_to(m_new, (tq, LANES))
        l_scr[c] = jnp.broadcast_to(jnp.sum(p, axis=-1, keepdims=True) * (1.0 / LANES), (tq, LANES))
        acc_scr[c] = _dot(p.astype(BF16), vm_ref[...])

    def update(c, rs, s, vb):
        m_prev = m_scr[c, rs, :]
        m_new = jnp.maximum(m_prev, jnp.max(s, axis=-1, keepdims=True))
        alpha = jnp.exp2(m_prev - m_new)
        p = jnp.exp2(s - jnp.tile(m_new, (1, s.shape[1] // LANES)))
        l_scr[c, rs, :] = alpha * l_scr[c, rs, :] + _lane_fold(p)
        acc_scr[c, rs, :] = (jnp.tile(alpha, (1, DIFF_DV // LANES)) * acc_scr[c, rs, :]
                             + _dot(p.astype(BF16), vb))
        m_scr[c, rs, :] = m_new

    def body(j, carry):
        rows = pl.ds(pl.multiple_of(j * tq, tq), tq)
        vb = v_ref[rows, :]
        for c in range(2):
            cs = slice(c * DIFF_HD, (c + 1) * DIFF_HD)
            kc = k_ref[rows, cs]
            for r in range(n_sub):
                rs = slice(r * sub, (r + 1) * sub)
                update(c, rs, _dot_nt(q_ref[rs, cs], kc), vb)
        return carry

    lax.fori_loop(0, qi, body, 0)

    base = pl.multiple_of(qi * tq, tq)
    for c in range(2):
        cs = slice(c * DIFF_HD, (c + 1) * DIFF_HD)
        for r in range(n_sub):
            rs = slice(r * sub, (r + 1) * sub)
            nk = (r + 1) * sub
            rows = pl.ds(base, nk)
            s = _dot_nt(q_ref[rs, cs], k_ref[rows, cs])
            q_pos = lax.broadcasted_iota(jnp.int32, (sub, nk), 0) + r * sub
            k_pos = lax.broadcasted_iota(jnp.int32, (sub, nk), 1)
            update(c, rs, jnp.where(k_pos <= q_pos, s, NEG_BIG), v_ref[rows, :])

    lam = _lambda(lq1, lk1, lq2, lk2, lam_init)
    a1 = acc_scr[0] / jnp.sum(l_scr[0], axis=-1, keepdims=True)
    a2 = acc_scr[1] / jnp.sum(l_scr[1], axis=-1, keepdims=True)
    o = a1 - lam * a2
    o_ref[...] = (o * _rms(o) * nw_ref[...] * (1.0 - lam_init)).astype(BF16)


def _diff_prompt(q, k, v, km, vm, lams, nw, lam_init, tq, sub):
    b, s, _ = q.shape
    hw = 2 * DIFF_HD
    small = lambda a: pl.BlockSpec(a.shape, lambda i, h, j: (0, 0))
    return pl.pallas_call(
        functools.partial(_diff_prompt_kernel, tq=tq, sub=sub, lam_init=lam_init),
        grid=(b, DIFF_HEADS, s // tq),
        in_specs=[pl.BlockSpec((None, tq, hw), lambda i, h, j: (i, j, h)),
                  pl.BlockSpec((None, s, hw), lambda i, h, j: (i, 0, h)),
                  pl.BlockSpec((None, s, DIFF_DV), lambda i, h, j: (i, 0, h)),
                  pl.BlockSpec((N_META, hw), lambda i, h, j: (0, h)),
                  pl.BlockSpec((N_META, DIFF_DV), lambda i, h, j: (0, h)),
                  small(lams[0]), small(lams[1]), small(lams[2]), small(lams[3]), small(nw)],
        out_specs=pl.BlockSpec((None, tq, DIFF_DV), lambda i, h, j: (i, j, h)),
        out_shape=jax.ShapeDtypeStruct((b, s, DIFF_HEADS * DIFF_DV), BF16),
        scratch_shapes=[pltpu.VMEM((2, tq, LANES), F32), pltpu.VMEM((2, tq, LANES), F32),
                        pltpu.VMEM((2, tq, DIFF_DV), F32)],
        compiler_params=pltpu.CompilerParams(
            dimension_semantics=("parallel", "parallel", "arbitrary"),
            vmem_limit_bytes=VMEM_LIMIT),
        name="diff_prompt",
    )(q, k, v, km, vm, *lams, nw)


Q_ROWS = 8
NEW_PAD = 16


def _decode_init(m_scr, l_scr, acc_scr):
    m_scr[...] = jnp.full(m_scr.shape, NEG_BIG, F32)
    l_scr[...] = jnp.zeros(l_scr.shape, F32)
    acc_scr[...] = jnp.zeros(acc_scr.shape, F32)


def _decode_update(m_scr, l_scr, acc_scr, s, vb):
    m_prev = m_scr[...][:, :1]
    m_new = jnp.maximum(m_prev, jnp.max(s, axis=-1, keepdims=True))
    alpha = jnp.exp2(m_prev - m_new)
    p = jnp.exp2(s - m_new)
    l_scr[...] = alpha * l_scr[...] + jnp.sum(p, axis=-1, keepdims=True)
    acc_scr[...] = alpha * acc_scr[...] + _dot(p.astype(BF16), vb)
    m_scr[...] = jnp.broadcast_to(m_new, m_scr.shape)


def _decode_pages(q_ref, k_refs, v_refs, m_scr, l_scr, acc_scr):
    def tile_row(ref, r):
        return ref[pl.ds(r, PAGE_SIZE, stride=TOKEN_ROWS), :].astype(BF16)

    def k_page(ref):
        return jnp.concatenate([tile_row(ref, hc) for hc in range(TOKEN_ROWS)], axis=1)

    def v_page(ref):
        return jnp.concatenate([tile_row(ref, half * DIFF_HEADS + h) for h in range(DIFF_HEADS)
                                for half in range(DIFF_DV // LANES)], axis=1)

    kg = jnp.concatenate([k_page(r) for r in k_refs], axis=0)
    vg = jnp.concatenate([v_page(r) for r in v_refs], axis=0)
    s = _dot_nt(q_ref[...], kg)
    _decode_update(m_scr, l_scr, acc_scr, s, vg)


def _decode_finish(q_ref, kn_ref, vn_ref, lam_refs, nw_ref, o_ref, m_scr, l_scr, acc_scr,
                   *, n_new, lam_init):
    s = _dot_nt(q_ref[...], kn_ref[...])
    t = lax.broadcasted_iota(jnp.int32, s.shape, 0) % Q_ROWS
    j = lax.broadcasted_iota(jnp.int32, s.shape, 1)
    s = jnp.where(j <= jnp.minimum(t, n_new - 1), s, NEG_BIG)
    _decode_update(m_scr, l_scr, acc_scr, s, vn_ref[...])
    lam = _lambda(*lam_refs, lam_init)
    out = acc_scr[...] / l_scr[...][:, :1]
    for h in range(DIFF_HEADS):
        r0 = h * 2 * Q_ROWS
        cols = slice(h * DIFF_DV, (h + 1) * DIFF_DV)
        o = out[r0:r0 + Q_ROWS, cols] - lam * out[r0 + Q_ROWS:r0 + 2 * Q_ROWS, cols]
        o_ref[:, cols] = o * _rms(o) * nw_ref[...] * (1.0 - lam_init)


def _diff_sample_kernel(pt_ref, q_ref, *refs, group, n_new, lam_init):
    del pt_ref
    k_refs = refs[:group]
    v_refs = refs[group:2 * group]
    kn_ref, vn_ref, lq1, lk1, lq2, lk2, nw_ref, o_ref, m_scr, l_scr, acc_scr = refs[2 * group:]
    g = pl.program_id(1)
    scr = (m_scr, l_scr, acc_scr)
    pl.when(g == 0)(functools.partial(_decode_init, *scr))
    _decode_pages(q_ref, k_refs, v_refs, *scr)

    @pl.when(g == pl.num_programs(1) - 1)
    def _():
        _decode_finish(q_ref, kn_ref, vn_ref, (lq1, lk1, lq2, lk2), nw_ref, o_ref, *scr,
                       n_new=n_new, lam_init=lam_init)


def _diff_sample(page_table, qbd, cache_k, cache_v, k_new, v_new, lams, nw, lam_init, n_new, group):
    bd, n_pages = page_table.shape
    nrow = qbd.shape[1]
    vw = DIFF_HEADS * DIFF_DV
    assert n_pages % group == 0

    def page_map(b, g, pt, i):
        return (pt[b, g * group + i], 0)

    per_b = lambda a: pl.BlockSpec((None,) + a.shape[1:], lambda b, g, pt: (b, 0, 0))
    small = lambda a: pl.BlockSpec(a.shape, lambda b, g, pt: (0, 0))
    page_specs = [pl.BlockSpec((PAGE_SIZE * TOKEN_ROWS, LANES), functools.partial(page_map, i=i))
                  for i in range(group)]
    in_specs = ([per_b(qbd)] + page_specs + page_specs
                + [per_b(k_new), per_b(v_new)] + [small(a) for a in lams] + [small(nw)])
    return pl.pallas_call(
        functools.partial(_diff_sample_kernel, group=group, n_new=n_new, lam_init=lam_init),
        grid_spec=pltpu.PrefetchScalarGridSpec(
            num_scalar_prefetch=1,
            grid=(bd, n_pages // group),
            in_specs=in_specs,
            out_specs=pl.BlockSpec((None, Q_ROWS, vw), lambda b, g, pt: (b, 0, 0)),
            scratch_shapes=[pltpu.VMEM((nrow, LANES), F32), pltpu.VMEM((nrow, LANES), F32),
                            pltpu.VMEM((nrow, vw), F32)]),
        out_shape=jax.ShapeDtypeStruct((bd, Q_ROWS, vw), F32),
        compiler_params=pltpu.CompilerParams(dimension_semantics=("parallel", "arbitrary"),
                                             vmem_limit_bytes=VMEM_LIMIT),
        name="diff_sample",
    )(page_table, qbd, *([cache_k] * group), *([cache_v] * group), k_new, v_new, *lams, nw)


def _out_proj_kernel(x_ref, mg_ref, md_ref, wo_ref, n2_ref, x1_ref, h2_ref):
    half = mg_ref.shape[1]
    y = x_ref[...] + _dot(mg_ref[...], wo_ref[:half, :]) + _dot(md_ref[...], wo_ref[half:, :])
    x1_ref[...] = y
    h2_ref[...] = (y * _rms(y) * n2_ref[...]).astype(BF16)


def _out_proj(x, mg, md, wo, n2, tm):
    r, d = x.shape
    row = lambda a: pl.BlockSpec((tm, a.shape[1]), lambda i: (i, 0))
    full = lambda a: pl.BlockSpec(a.shape, lambda i: (0, 0))
    return pl.pallas_call(
        _out_proj_kernel,
        grid=(r // tm,),
        in_specs=[row(x), row(mg), row(md), full(wo), full(n2)],
        out_specs=[row(x), row(x)],
        out_shape=[jax.ShapeDtypeStruct((r, d), F32), jax.ShapeDtypeStruct((r, d), BF16)],
        compiler_params=pltpu.CompilerParams(dimension_semantics=("parallel",),
                                             vmem_limit_bytes=VMEM_LIMIT),
        name="out_proj",
    )(x, mg, md, wo, n2)


def _ffn_kernel(h_ref, wg_ref, wu_ref, wd_ref, x_ref, fw_ref, y_ref, acc_scr):
    j = pl.program_id(1)

    @pl.when(j == 0)
    def _():
        acc_scr[...] = jnp.zeros(acc_scr.shape, F32)

    h = h_ref[...]
    act = (_silu(_dot(h, wg_ref[...])) * _dot(h, wu_ref[...])).astype(BF16)
    acc_scr[...] += _dot(act, wd_ref[...])

    @pl.when(j == pl.num_programs(1) - 1)
    def _():
        y = x_ref[...] + acc_scr[...]
        y_ref[...] = y * _rms(y) * fw_ref[...]


def _ffn(h2, wg, wu, wd, x1, fw, tm, tf):
    r, d = h2.shape
    f = wg.shape[1]
    return pl.pallas_call(
        _ffn_kernel,
        grid=(r // tm, f // tf),
        in_specs=[pl.BlockSpec((tm, d), lambda i, j: (i, 0)),
                  pl.BlockSpec((d, tf), lambda i, j: (0, j)),
                  pl.BlockSpec((d, tf), lambda i, j: (0, j)),
                  pl.BlockSpec((tf, d), lambda i, j: (j, 0)),
                  pl.BlockSpec((tm, d), lambda i, j: (i, 0)),
                  pl.BlockSpec(fw.shape, lambda i, j: (0, 0))],
        out_specs=pl.BlockSpec((tm, d), lambda i, j: (i, 0)),
        out_shape=jax.ShapeDtypeStruct((r, d), F32),
        scratch_shapes=[pltpu.VMEM((tm, d), F32)],
        compiler_params=pltpu.CompilerParams(dimension_semantics=("parallel", "arbitrary"),
                                             vmem_limit_bytes=VMEM_LIMIT),
        name="ffn",
    )(h2, wg, wu, wd, x1, fw)


def _out_ffn_kernel(x_ref, mg_ref, md_ref, wo_ref, n2_ref, wg_ref, wu_ref, wd_ref, fw_ref, y_ref,
                    acc_scr, h_scr):
    j = pl.program_id(1)

    @pl.when(j == 0)
    def _():
        half = mg_ref.shape[1]
        x1 = x_ref[...] + _dot(mg_ref[...], wo_ref[:half, :]) + _dot(md_ref[...], wo_ref[half:, :])
        acc_scr[...] = x1
        h_scr[...] = (x1 * _rms(x1) * n2_ref[...]).astype(BF16)

    h = h_scr[...]
    act = (_silu(_dot(h, wg_ref[...])) * _dot(h, wu_ref[...])).astype(BF16)
    acc_scr[...] += _dot(act, wd_ref[...])

    @pl.when(j == pl.num_programs(1) - 1)
    def _():
        y = acc_scr[...]
        y_ref[...] = y * _rms(y) * fw_ref[...]


def _out_ffn(x, mg, md, wo, n2, wg, wu, wd, fw, tm, tf):
    r, d = x.shape
    f = wg.shape[1]
    row = lambda a: pl.BlockSpec((tm, a.shape[1]), lambda i, j: (i, 0))
    full = lambda a: pl.BlockSpec(a.shape, lambda i, j: (0, 0))
    return pl.pallas_call(
        _out_ffn_kernel,
        grid=(r // tm, f // tf),
        in_specs=[row(x), row(mg), row(md),
                  pl.BlockSpec(wo.shape, lambda i, j: (0, 0), pipeline_mode=pl.Buffered(1)),
                  full(n2),
                  pl.BlockSpec((d, tf), lambda i, j: (0, j)),
                  pl.BlockSpec((d, tf), lambda i, j: (0, j)),
                  pl.BlockSpec((tf, d), lambda i, j: (j, 0)),
                  full(fw)],
        out_specs=row(x),
        out_shape=jax.ShapeDtypeStruct((r, d), F32),
        scratch_shapes=[pltpu.VMEM((tm, d), F32), pltpu.VMEM((tm, d), BF16)],
        compiler_params=pltpu.CompilerParams(dimension_semantics=("parallel", "arbitrary"),
                                             vmem_limit_bytes=VMEM_LIMIT),
        name="out_ffn",
    )(x, mg, md, wo, n2, wg, wu, wd, fw)


def _rope_tables(pos):
    half = ROT_DIM // 2
    inv = ROPE_THETA ** (-jnp.arange(half, dtype=F32) * 2.0 / ROT_DIM)
    ang = pos.astype(F32)[:, None] * inv[None, :]
    cos, sin = jnp.cos(ang), jnp.sin(ang)
    n = pos.shape[0]
    zeros = lambda w: jnp.zeros((n, w), F32)
    c = jnp.concatenate([cos, cos, jnp.ones((n, LANES - ROT_DIM), F32)], axis=1)
    s_lo = jnp.concatenate([zeros(half), sin, zeros(LANES - ROT_DIM)], axis=1)
    s_hi = jnp.concatenate([-sin, zeros(LANES - half)], axis=1)
    return c, s_lo, s_hi


def _value_rows_to_heads(v2d, n_tok):
    halves = DIFF_DV // LANES
    v = v2d.reshape(n_tok, halves, DIFF_HEADS, LANES).transpose(0, 2, 1, 3)
    return v.reshape(n_tok, DIFF_HEADS, DIFF_DV)


def kernel(x_prompt, x_sample, cache_k, cache_v, state_gla, page_table, meta_tokens, norm1_w, w_in,
           w_a2, b_a, gla_norm_w, lambda_q1, lambda_k1, lambda_q2, lambda_k2, diff_norm_w, w_o,
           norm2_w, w_gate, w_up, w_down, final_norm_w):
    b, s, d = x_prompt.shape
    bd, td, _ = x_sample.shape
    assert norm1_w.shape[0] == 1, "single-layer step only"
    assert td <= Q_ROWS and td <= N_META
    n_phys = cache_k.shape[1]
    past_len = page_table.shape[1] * PAGE_SIZE
    lam_init = 0.8 - 0.6 * math.exp(-0.3 * 0)
    n_small = bd * td
    n_tok = N_META + s

    g0 = 2 * GLA_HEADS * GLA_DK + GLA_HEADS * GLA_DV
    w_main, w_a1 = _regroup_w_in(w_in[0], g0, GLA_GATE_RANK, tr=256)
    w_a2p = jnp.pad(w_a2[0], ((0, LANES - GLA_GATE_RANK), (0, 0))).astype(BF16)
    row2 = lambda a: a.reshape(1, -1).astype(F32)
    n1, ba, gnw, dnw, n2, fw = (row2(norm1_w[0]), row2(b_a[0]), row2(gla_norm_w[0]),
                                row2(diff_norm_w[0]), row2(norm2_w[0]), row2(final_norm_w))
    lams = tuple(row2(a[0]) for a in (lambda_q1, lambda_k1, lambda_q2, lambda_k2))
    wo, wg, wu, wd = (w_o[0].astype(BF16), w_gate[0].astype(BF16), w_up[0].astype(BF16),
                      w_down[0].astype(BF16))

    x_main = x_prompt.reshape(b * s, d)
    x_new = x_sample.reshape(n_small, d)
    x_small = jnp.concatenate([x_new, meta_tokens.astype(F32)], axis=0)
    pos_small = jnp.concatenate([past_len + jnp.arange(n_small) % td, jnp.arange(N_META)])

    tm = 256
    kv_main = pl.BlockSpec((pl.Element(tm * TOKEN_ROWS), pl.Element(LANES)),
                           lambda i, j: ((i * n_tok + N_META + j * tm) * TOKEN_ROWS, 0))
    proj_main = _proj(x_main, n1, w_main, w_a1, w_a2p, ba, *_rope_tables(N_META + jnp.arange(s)),
                      tm=tm, n_seq=b, kv_rows=b * n_tok * TOKEN_ROWS, kv_spec=kv_main)
    r_small = n_small + N_META
    kv_small = pl.BlockSpec((r_small * TOKEN_ROWS, LANES), lambda i, j: (0, 0))
    proj_small = _proj(x_small, n1, w_main, w_a1, w_a2p, ba, *_rope_tables(pos_small),
                       tm=r_small, n_seq=1, kv_rows=r_small * TOKEN_ROWS, kv_spec=kv_small)
    qg, kg, vg, gk, go, qd, kd, vd, kf, vf = proj_main
    qg_s, kg_s, vg_s, gk_s, go_s, qd_s, kd_s, vd_s, kf_s, vf_s = proj_small

    def small_seq(a):
        new = jnp.pad(a[:n_small].reshape(bd, td, -1), ((0, 0), (0, N_META - td), (0, 0)))
        return jnp.concatenate([new, a[n_small:][None]], axis=0)

    s0_small = jnp.concatenate([state_gla[0].astype(F32),
                                jnp.zeros((1,) + state_gla.shape[2:], F32)], axis=0)
    mg_small, st_small = _gla(small_seq(qg_s), small_seq(kg_s), small_seq(vg_s), small_seq(gk_s),
                              small_seq(go_s), s0_small, gnw,
                              chunk=N_META, n_chunks=1, n_seq=3)
    seq = lambda a: a.reshape(b, s, -1)
    s0_main = jnp.broadcast_to(st_small[bd:], (b,) + st_small.shape[1:])
    mg_main, st_main = _gla(seq(qg), seq(kg), seq(vg), seq(gk), seq(go), s0_main, gnw,
                            chunk=GLA_CHUNK, n_chunks=8, n_seq=b, unroll=4)

    md_main = _diff_prompt(seq(qd), seq(kd), seq(vd), kd_s[n_small:], vd_s[n_small:], lams, dnw,
                           lam_init, tq=2048, sub=128)

    n_sub = 2 * DIFF_HEADS
    q4 = jnp.pad(qd_s[:n_small].reshape(bd, td, n_sub, DIFF_HD),
                 ((0, 0), (0, Q_ROWS - td), (0, 0), (0, 0))).transpose(0, 2, 1, 3)
    qbd = (q4[:, :, :, None, :] * jnp.eye(n_sub, dtype=BF16)[None, :, None, :, None])
    qbd = qbd.reshape(bd, n_sub * Q_ROWS, n_sub * DIFF_HD)
    pad_new = lambda a: jnp.pad(a[:n_small].reshape(bd, td, -1), ((0, 0), (0, NEW_PAD - td), (0, 0)))
    halves = DIFF_DV // LANES
    cache_k2 = cache_k[0].reshape(n_phys * PAGE_SIZE * TOKEN_ROWS, LANES)
    cache_v2 = (cache_v[0].reshape(n_phys * PAGE_SIZE, DIFF_HEADS, halves, LANES).transpose(0, 2, 1, 3)
                .reshape(n_phys * PAGE_SIZE * TOKEN_ROWS, LANES))
    od_small = _diff_sample(page_table, qbd, cache_k2, cache_v2, pad_new(kd_s), pad_new(vd_s), lams,
                            dnw, lam_init, n_new=td, group=16)
    md_small = od_small[:, :td].reshape(n_small, -1).astype(BF16)

    y_main = _out_ffn(x_main, mg_main.reshape(b * s, -1), md_main.reshape(b * s, -1), wo, n2,
                      wg, wu, wd, fw, tm=512, tf=512)
    x1_small, h2_small = _out_proj(x_new, mg_small[:bd, :td].reshape(n_small, -1), md_small, wo, n2,
                                   tm=n_small)
    y_small = _ffn(h2_small, wg, wu, wd, x1_small, fw, tm=n_small, tf=512)

    meta_rows = slice(n_small * TOKEN_ROWS, r_small * TOKEN_ROWS)
    for i in range(b):
        at = (i * n_tok * TOKEN_ROWS, 0)
        kf = lax.dynamic_update_slice(kf, kf_s[meta_rows], at)
        vf = lax.dynamic_update_slice(vf, vf_s[meta_rows], at)
    k_prompt = kf.reshape(1, b, n_tok, DIFF_HEADS, 2, DIFF_HD)
    v_prompt = _value_rows_to_heads(vf, b * n_tok).reshape(1, b, n_tok, DIFF_HEADS, DIFF_DV)
    gla_prompt = st_main[None].astype(state_gla.dtype)
    k_sample = kf_s[:n_small * TOKEN_ROWS].reshape(1, bd, td, DIFF_HEADS, 2, DIFF_HD)
    v_sample = _value_rows_to_heads(vf_s[:n_small * TOKEN_ROWS], n_small).reshape(
        1, bd, td, DIFF_HEADS, DIFF_DV)
    gla_sample = st_small[:bd][None].astype(state_gla.dtype)
    return (y_main.reshape(b, s, d), y_small.reshape(bd, td, d), k_prompt, v_prompt, gla_prompt,
            k_sample, v_sample, gla_sample)
```
